```python
import math
import jax
import jax.numpy as jnp
from jax import lax
import numpy as np

D_MODEL = 1024
BATCH = 4
SEQ = 4096
DEPTH = 2
DEC_BATCH = 32
DEC_SEQ = 1
PAST_LEN = 8192
PAGE_SIZE = 128

DH_SB = 64
H_SB = D_MODEL // (2 * DH_SB)
DH_DF = 64
DV_DF = 2 * DH_DF
H_DF = D_MODEL // (4 * DH_DF)
DK_GD = 128
DV_GD = 128
H_GD = D_MODEL // (2 * DK_GD)
CONV_W = 4
GD_CONV_CH = 2 * H_GD * DK_GD + H_GD * DV_GD
GDN_CHUNK = 64
N_BRANCH = 3
BRANCH_W = H_SB * DH_SB
D_FF = (8 * D_MODEL // 3 + 255) // 256 * 256
Q_BLOCK = 128
NORM_EPS = 1e-6
L2_EPS = 1e-6
_IN_SIZES = (H_SB * DH_SB, H_SB * DH_SB, H_SB * DH_SB,
             H_DF * 2 * DH_DF, H_DF * 2 * DH_DF, H_DF * DV_DF,
             H_GD * DK_GD, H_GD * DK_GD, H_GD * DV_GD, H_GD, H_GD, H_GD * DV_GD,
             N_BRANCH * D_MODEL)
IN_COLS = sum(_IN_SIZES)

kernel_name = 'hybrid_stickbreak_diffattn_gdn_macaron_step'


def _split_points():
    return [int(v) for v in np.cumsum(_IN_SIZES)[:-1]]


def _rmsnorm(x, w):
    xf = x.astype(jnp.float32)
    y = xf * lax.rsqrt(jnp.mean(xf * xf, axis=-1, keepdims=True) + NORM_EPS)
    return (y * w.astype(jnp.float32)).astype(x.dtype)


def _l2norm(x):
    return x * lax.rsqrt(jnp.sum(x * x, axis=-1, keepdims=True) + L2_EPS)


def _swiglu(h, w_in, w_out):
    g, u = jnp.split(h @ w_in, 2, axis=-1)
    return (jax.nn.silu(g) * u) @ w_out


def _over_query_blocks(fn, q, q_pos):
    b, t = q.shape[0], q.shape[1]
    if t <= Q_BLOCK:
        return fn(q, q_pos)
    nb = -(-t // Q_BLOCK)
    pad = nb * Q_BLOCK - t
    qp = jnp.pad(q, ((0, 0), (0, pad)) + ((0, 0),) * (q.ndim - 2))
    pp = jnp.pad(q_pos, (0, pad), mode='edge')
    qb = jnp.moveaxis(qp.reshape((b, nb, Q_BLOCK) + q.shape[2:]), 1, 0)
    out = lax.map(lambda a: fn(a[0], a[1]), (qb, pp.reshape(nb, Q_BLOCK)))
    out = jnp.moveaxis(out, 0, 1)
    return out.reshape((b, nb * Q_BLOCK) + out.shape[3:])[:, :t]


def _stick_breaking(q, k, v, q_pos, k_pos):
    scale = DH_SB ** -0.5
    kf = k.astype(jnp.float32)
    vf = v.astype(jnp.float32)

    def block(qb, pb):
        z = jnp.einsum('bqhd,bkhd->bhqk', qb.astype(jnp.float32), kf) * scale
        mask = k_pos[None, :] < pb[:, None]
        log_1m = jnp.where(mask, jax.nn.log_sigmoid(-z), 0.0)
        tail = lax.cumsum(log_1m, axis=3, reverse=True) - log_1m
        a = jnp.where(mask, jnp.exp(jax.nn.log_sigmoid(z) + tail), 0.0)
        return jnp.einsum('bhqk,bkhd->bqhd', a, vf)

    return _over_query_blocks(block, q, q_pos).astype(q.dtype)


def _diff_attention(q, k, v, q_pos, k_pos, lam):
    scale = DH_DF ** -0.5
    slopes = 2.0 ** (-8.0 * jnp.arange(1, H_DF + 1, dtype=jnp.float32) / H_DF)
    kf = k.astype(jnp.float32)
    vf = v.astype(jnp.float32)

    def block(qb, pb):
        s = jnp.einsum('bqhnd,bkhnd->bhnqk', qb.astype(jnp.float32), kf) * scale
        dist = (pb[:, None] - k_pos[None, :]).astype(jnp.float32)
        s = s - slopes[:, None, None, None] * dist
        p = jax.nn.softmax(jnp.where(dist >= 0, s, -jnp.inf), axis=-1)
        w = p[:, :, 0] - lam * p[:, :, 1]
        return jnp.einsum('bhqk,bkhe->bqhe', w, vf)

    return _over_query_blocks(block, q, q_pos).astype(q.dtype)


def _causal_conv(xp, w):
    return lax.conv_general_dilated(
        xp, w[:, None, :].astype(xp.dtype), window_strides=(1,), padding='VALID',
        dimension_numbers=('NWC', 'WIO', 'NWC'), feature_group_count=xp.shape[-1])


def _gated_delta(q, k, v, g, beta, s0):
    b, t, h, _ = q.shape
    dv = v.shape[-1]
    c = min(GDN_CHUNK, t)
    nc = -(-t // c)
    pad = nc * c - t

    def chunks(a):
        a = jnp.pad(a, ((0, 0), (0, pad)) + ((0, 0),) * (a.ndim - 2))
        a = jnp.moveaxis(a.reshape((b, nc, c) + a.shape[2:]), 1, 0)
        return jnp.swapaxes(a, 2, 3)

    tri = jnp.tril(jnp.ones((c, c), dtype=bool))
    strict = jnp.tril(jnp.ones((c, c), dtype=bool), -1)
    eye = jnp.eye(c, dtype=jnp.float32)

    def step(s, inp):
        qc, kc, vc, gc, bc = inp
        G = lax.cumsum(gc, axis=2)
        decay = jnp.exp(jnp.where(tri, G[..., :, None] - G[..., None, :], -jnp.inf))
        kk = jnp.einsum('bhtd,bhid->bhti', kc, kc)
        a_mat = jnp.where(strict, bc[..., :, None] * decay * kk, 0.0) + eye
        rhs = jnp.concatenate([bc[..., None] * vc, (bc * jnp.exp(G))[..., None] * kc], axis=-1)
        sol = lax.linalg.triangular_solve(a_mat, rhs, left_side=True, lower=True,
                                          unit_diagonal=True)
        u = sol[..., :dv] - sol[..., dv:] @ s
        qk = jnp.einsum('bhtd,bhid->bhti', qc, kc) * decay
        o = (jnp.einsum('bhtd,bhdv->bhtv', qc * jnp.exp(G)[..., None], s)
             + jnp.einsum('bhti,bhiv->bhtv', qk, u))
        g_last = G[..., -1:]
        s_new = (jnp.exp(g_last)[..., None] * s
                 + jnp.einsum('bhid,bhiv->bhdv', kc * jnp.exp(g_last - G)[..., None], u))
        return s_new, o

    s_fin, o = lax.scan(step, s0, (chunks(q), chunks(k), chunks(v), chunks(g), chunks(beta)))
    o = jnp.transpose(o, (1, 0, 3, 2, 4)).reshape(b, nc * c, h, dv)[:, :t]
    return o, s_fin


def _mixer(h, l, q_pos, past, params):
    (_, _, _, w_in, diff_lambda, diff_norm_w, gdn_conv_w, gdn_a_log, gdn_dt_bias,
     gdn_norm_w, w_branch, w_out) = params
    f32 = jnp.float32
    b, t, _ = h.shape
    (sq, sk, sv, dq, dk, dv, gq, gk, gv, ga, gb, gz, gates) = jnp.split(
        h @ w_in[l], _split_points(), axis=-1)
    sq = sq.reshape(b, t, H_SB, DH_SB)
    sk = sk.reshape(b, t, H_SB, DH_SB)
    sv = sv.reshape(b, t, H_SB, DH_SB)
    dq = dq.reshape(b, t, H_DF, 2, DH_DF)
    dk = dk.reshape(b, t, H_DF, 2, DH_DF)
    dv = dv.reshape(b, t, H_DF, DV_DF)
    xc = jnp.concatenate([gq, gk, gv], axis=-1)

    if past is None:
        sk_all, sv_all, dk_all, dv_all = sk, sv, dk, dv
        k_pos = q_pos
        conv_buf = jnp.zeros((b, CONV_W - 1, GD_CONV_CH), h.dtype)
        s0 = jnp.zeros((b, H_GD, DK_GD, DV_GD), f32)
    else:
        p_sk, p_sv, p_dk, p_dv, s0, conv_buf = past
        cat = lambda p, n: jnp.concatenate([p.astype(n.dtype), n], axis=1)
        sk_all, sv_all, dk_all, dv_all = cat(p_sk, sk), cat(p_sv, sv), cat(p_dk, dk), cat(p_dv, dv)
        k_pos = jnp.arange(p_sk.shape[1] + t, dtype=jnp.int32)
        conv_buf = conv_buf.astype(h.dtype)
        s0 = s0.astype(f32)

    o_sb = _stick_breaking(sq, sk_all, sv_all, q_pos, k_pos).reshape(b, t, BRANCH_W)

    lam_init = 0.8 - 0.6 * math.exp(-0.3 * l)
    lv = diff_lambda[l].astype(f32)
    lam = jnp.exp(jnp.sum(lv[0] * lv[1])) - jnp.exp(jnp.sum(lv[2] * lv[3])) + lam_init
    o_df = _diff_attention(dq, dk_all, dv_all, q_pos, k_pos, lam)
    o_df = (_rmsnorm(o_df, diff_norm_w[l]) * (1.0 - lam_init)).reshape(b, t, BRANCH_W)

    xpad = jnp.concatenate([conv_buf, xc], axis=1)
    new_buf = xpad[:, -(CONV_W - 1):]
    xconv = jax.nn.silu(_causal_conv(xpad, gdn_conv_w[l])).astype(f32)
    cq, ck, cv = jnp.split(xconv, [H_GD * DK_GD, 2 * H_GD * DK_GD], axis=-1)
    cq = _l2norm(cq.reshape(b, t, H_GD, DK_GD)) * (DK_GD ** -0.5)
    ck = _l2norm(ck.reshape(b, t, H_GD, DK_GD))
    cv = cv.reshape(b, t, H_GD, DV_GD)
    beta = jax.nn.sigmoid(gb.astype(f32))
    g = -jnp.exp(gdn_a_log[l].astype(f32)) * jax.nn.softplus(
        ga.astype(f32) + gdn_dt_bias[l].astype(f32))
    o_gd, s_new = _gated_delta(cq, ck, cv, g, beta, s0)
    o_gd = _rmsnorm(o_gd.astype(h.dtype), gdn_norm_w[l]) * jax.nn.silu(
        gz.reshape(b, t, H_GD, DV_GD))
    o_gd = o_gd.reshape(b, t, BRANCH_W)

    branches = jnp.stack([o_sb, o_df, o_gd], axis=2)
    proj = jnp.einsum('btnc,ncd->btnd', branches, w_branch[l])
    gate = jax.nn.sigmoid(gates.reshape(b, t, N_BRANCH, D_MODEL))
    out = jnp.sum(gate * proj, axis=2) @ w_out[l]
    return out, (sk, sv, dk, dv, s_new.astype(h.dtype), new_buf)


def _layer(x, l, q_pos, past, params):
    norm_w, ffn_w_in, ffn_w_out = params[0], params[1], params[2]
    x = x + 0.5 * _swiglu(_rmsnorm(x, norm_w[l, 0]), ffn_w_in[l, 0], ffn_w_out[l, 0])
    m, new_state = _mixer(_rmsnorm(x, norm_w[l, 1]), l, q_pos, past, params)
    x = x + m
    x = x + 0.5 * _swiglu(_rmsnorm(x, norm_w[l, 2]), ffn_w_in[l, 1], ffn_w_out[l, 1])
    return x, new_state


def _gather_pages(pool, page_table):
    rows = pool[page_table]
    return rows.reshape((page_table.shape[0], page_table.shape[1] * pool.shape[1]) + pool.shape[2:])


def setup_inputs(seed: int = 0) -> dict:
    key = jax.random.key(seed)
    ks = jax.random.split(key, 24)
    f32 = jnp.float32
    n_pages = PAST_LEN // PAGE_SIZE
    n_used = DEC_BATCH * n_pages
    n_pool = n_used + n_used // 4
    nrm = lambda k, shape, s=1.0: s * jax.random.normal(k, shape, f32)
    page_table = jax.random.permutation(ks[8], n_pool)[:n_used].reshape(
        DEC_BATCH, n_pages).astype(jnp.int32)
    dt = jnp.exp(jax.random.uniform(ks[17], (DEPTH, H_GD), f32,
                                    minval=math.log(1e-3), maxval=math.log(1e-1)))
    return {
        'x_prompt': nrm(ks[0], (BATCH, SEQ, D_MODEL)),
        'x_sample': nrm(ks[1], (DEC_BATCH, DEC_SEQ, D_MODEL)),
        'cache_sb_k': nrm(ks[2], (DEPTH, n_pool, PAGE_SIZE, H_SB, DH_SB)),
        'cache_sb_v': nrm(ks[3], (DEPTH, n_pool, PAGE_SIZE, H_SB, DH_SB)),
        'cache_df_k': nrm(ks[4], (DEPTH, n_pool, PAGE_SIZE, H_DF, 2, DH_DF)),
        'cache_df_v': nrm(ks[5], (DEPTH, n_pool, PAGE_SIZE, H_DF, DV_DF)),
        'state_gdn': nrm(ks[6], (DEPTH, DEC_BATCH, H_GD, DK_GD, DV_GD), 0.1),
        'state_gdn_conv': nrm(ks[7], (DEPTH, DEC_BATCH, CONV_W - 1, GD_CONV_CH)),
        'page_table': page_table,
        'norm_w': 1.0 + nrm(ks[9], (DEPTH, 3, D_MODEL), 0.02),
        'ffn_w_in': nrm(ks[10], (DEPTH, 2, D_MODEL, 2 * D_FF), D_MODEL ** -0.5),
        'ffn_w_out': nrm(ks[11], (DEPTH, 2, D_FF, D_MODEL), D_FF ** -0.5),
        'w_in': nrm(ks[12], (DEPTH, D_MODEL, IN_COLS), D_MODEL ** -0.5),
        'diff_lambda': nrm(ks[13], (DEPTH, 4, DH_DF), 0.1),
        'diff_norm_w': 1.0 + nrm(ks[14], (DEPTH, DV_DF), 0.02),
        'gdn_conv_w': nrm(ks[15], (DEPTH, CONV_W, GD_CONV_CH), CONV_W ** -0.5),
        'gdn_a_log': jnp.log(jax.random.uniform(ks[16], (DEPTH, H_GD), f32, minval=1.0, maxval=16.0)),
        'gdn_dt_bias': jnp.log(jnp.expm1(dt)),
        'gdn_norm_w': 1.0 + nrm(ks[18], (DEPTH, DV_GD), 0.02),
        'w_branch': nrm(ks[19], (DEPTH, N_BRANCH, BRANCH_W, D_MODEL), BRANCH_W ** -0.5),
        'w_out': nrm(ks[20], (DEPTH, D_MODEL, D_MODEL), D_MODEL ** -0.5),
        'final_norm_w': 1.0 + nrm(ks[21], (D_MODEL,), 0.02),
    }


def reference(x_prompt, x_sample, cache_sb_k, cache_sb_v, cache_df_k, cache_df_v,
              state_gdn, state_gdn_conv, page_table, norm_w, ffn_w_in, ffn_w_out, w_in,
              diff_lambda, diff_norm_w, gdn_conv_w, gdn_a_log, gdn_dt_bias, gdn_norm_w,
              w_branch, w_out, final_norm_w):
    params = (norm_w, ffn_w_in, ffn_w_out, w_in, diff_lambda, diff_norm_w, gdn_conv_w,
              gdn_a_log, gdn_dt_bias, gdn_norm_w, w_branch, w_out)

    xp = x_prompt
    pos_p = jnp.arange(x_prompt.shape[1], dtype=jnp.int32)
    p_states = []
    for l in range(DEPTH):
        xp, st = _layer(xp, l, pos_p, None, params)
        p_states.append(st)
    y_prompt = _rmsnorm(xp, final_norm_w)

    past_len = page_table.shape[1] * cache_sb_k.shape[2]
    xs = x_sample
    pos_s = past_len + jnp.arange(x_sample.shape[1], dtype=jnp.int32)
    s_states = []
    for l in range(DEPTH):
        past = (_gather_pages(cache_sb_k[l], page_table), _gather_pages(cache_sb_v[l], page_table),
                _gather_pages(cache_df_k[l], page_table), _gather_pages(cache_df_v[l], page_table),
                state_gdn[l], state_gdn_conv[l])
        xs, st = _layer(xs, l, pos_s, past, params)
        s_states.append(st)
    y_sample = _rmsnorm(xs, final_norm_w)

    p_sb_k, p_sb_v, p_df_k, p_df_v, p_gdn, p_conv = (jnp.stack(a) for a in zip(*p_states))
    s_sb_k, s_sb_v, s_df_k, s_df_v, s_gdn, s_conv = (jnp.stack(a) for a in zip(*s_states))
    return (y_prompt, y_sample, p_sb_k, p_sb_v, p_df_k, p_df_v, p_gdn, p_conv,
            s_sb_k, s_sb_v, s_df_k, s_df_v, s_gdn, s_conv)
```

```python
import functools
import math

import jax
import jax.numpy as jnp
from jax import lax
from jax.experimental import pallas as pl
from jax.experimental.pallas import tpu as pltpu

F32 = jnp.float32
BF16 = jnp.bfloat16

D_MODEL = 1024
DH_SB = 64
H_SB = 8
DH_DF = 64
H_DF = 4
DK_GD = 128
DV_GD = 128
H_GD = 4
CONV_W = 4
GD_CONV_CH = 2 * H_GD * DK_GD + H_GD * DV_GD
GDN_CHUNK = 64
BRANCH_W = 512
N_BRANCH = 3
D_FF = 2816
NORM_EPS = 1e-6
L2_EPS = 1e-6
NEG_BIG = -1e30

LANES = 128
FF_CHUNK = 256
GAB_W = LANES
VMEM_LIMIT = 56 * 1024 * 1024

PROJ_OUTS = (
    ("sq", 512, BF16), ("sk", 512, F32), ("sv", 512, F32),
    ("dq", 512, BF16), ("dk", 512, F32), ("dv", 512, F32),
    ("xc", GD_CONV_CH, F32), ("gz", 512, F32),
    ("gates", N_BRANCH * D_MODEL, F32), ("gab", GAB_W, F32),
)
PROJ_COLS = sum(w for _, w, _ in PROJ_OUTS)


def _dot(a, b, precision=None):
    return jnp.dot(a, b, preferred_element_type=F32, precision=precision)


def _dot_nt(a, b, precision=None):
    return lax.dot_general(a, b, (((1,), (1,)), ((), ())),
                           preferred_element_type=F32, precision=precision)


def _dot_tn(a, b, precision=None):
    return lax.dot_general(a, b, (((0,), (0,)), ((), ())),
                           preferred_element_type=F32, precision=precision)


_HI = lax.Precision.HIGHEST


def _rms(x, w):
    ms = jnp.mean(x * x, axis=-1, keepdims=True)
    return x * lax.rsqrt(ms + NORM_EPS) * w


def _sigmoid(x):
    return 1.0 / (1.0 + jnp.exp(-x))


def _silu(x):
    return x * _sigmoid(x)


def _softplus(x):
    return jnp.maximum(x, 0.0) + jnp.log(1.0 + jnp.exp(-jnp.abs(x)))


def _params(sem):
    return pltpu.CompilerParams(dimension_semantics=sem, vmem_limit_bytes=VMEM_LIMIT)


def _resident(shape):
    nd = len(shape)
    return pl.BlockSpec(shape, lambda *_: (0,) * nd, pipeline_mode=pl.Buffered(1))


def _ffn_kernel(x_ref, nw_ref, win_ref, wout_ref, fw_ref, o_ref, *, final):
    x = x_ref[...]
    h = _rms(x, nw_ref[...]).astype(BF16)
    acc = jnp.zeros(x.shape, F32)
    for c in range(D_FF // FF_CHUNK):
        lo = c * FF_CHUNK
        g = _dot(h, win_ref[:, lo:lo + FF_CHUNK])
        u = _dot(h, win_ref[:, D_FF + lo:D_FF + lo + FF_CHUNK])
        a = (_silu(g) * u).astype(BF16)
        acc = acc + _dot(a, wout_ref[lo:lo + FF_CHUNK, :])
    y = x + 0.5 * acc
    if final:
        y = _rms(y, fw_ref[...])
    o_ref[...] = y


def _ffn(x, nw, w_in, w_out, fw, *, final, tm):
    m = x.shape[0]
    row = pl.BlockSpec((tm, D_MODEL), lambda i: (i, 0))
    return pl.pallas_call(
        functools.partial(_ffn_kernel, final=final),
        out_shape=jax.ShapeDtypeStruct((m, D_MODEL), F32),
        grid=(m // tm,),
        in_specs=[row, _resident((1, D_MODEL)), _resident(w_in.shape),
                  _resident(w_out.shape), _resident((1, D_MODEL))],
        out_specs=row,
        compiler_params=_params(("parallel",)),
        name="ffn",
    )(x, nw, w_in, w_out, fw)


def _proj_kernel(x_ref, nw_ref, w_ref, *o_refs):
    h = _rms(x_ref[...], nw_ref[...]).astype(BF16)
    off = 0
    for (_, width, dt), o_ref in zip(PROJ_OUTS, o_refs):
        for lo in range(0, width, 512):
            n = min(512, width - lo)
            o_ref[:, lo:lo + n] = _dot(h, w_ref[:, off + lo:off + lo + n]).astype(dt)
        off += width


def _proj(x, nw, w, *, tm):
    m = x.shape[0]
    return pl.pallas_call(
        _proj_kernel,
        out_shape=[jax.ShapeDtypeStruct((m, wd), dt) for _, wd, dt in PROJ_OUTS],
        grid=(m // tm,),
        in_specs=[pl.BlockSpec((tm, D_MODEL), lambda i: (i, 0)),
                  _resident((1, D_MODEL)), _resident(w.shape)],
        out_specs=[pl.BlockSpec((tm, wd), lambda i: (i, 0)) for _, wd, _ in PROJ_OUTS],
        compiler_params=_params(("parallel",)),
        name="proj",
    )(x, nw, w)


def _merge_kernel(x_ref, osb_ref, odf_ref, ogd_ref, gates_ref, wb_ref, wo_ref, o_ref):
    m = None
    for i, b_ref in enumerate((osb_ref, odf_ref, ogd_ref)):
        p = _dot(b_ref[...], wb_ref[i])
        t = _sigmoid(gates_ref[:, i * D_MODEL:(i + 1) * D_MODEL]) * p
        m = t if m is None else m + t
    o_ref[...] = x_ref[...] + _dot(m.astype(BF16), wo_ref[...])


def _merge(x, o_sb, o_df, o_gd, gates, wb, wo, *, tm):
    m = x.shape[0]
    row = lambda w: pl.BlockSpec((tm, w), lambda i: (i, 0))
    return pl.pallas_call(
        _merge_kernel,
        out_shape=jax.ShapeDtypeStruct((m, D_MODEL), F32),
        grid=(m // tm,),
        in_specs=[row(D_MODEL), row(BRANCH_W), row(BRANCH_W), row(BRANCH_W),
                  row(N_BRANCH * D_MODEL), _resident(wb.shape), _resident(wo.shape)],
        out_specs=row(D_MODEL),
        compiler_params=_params(("parallel",)),
        name="merge",
    )(x, o_sb, o_df, o_gd, gates, wb, wo)


def _log_sigmoid_pair(z):
    ls = jnp.minimum(z, 0.0) - jnp.log(1.0 + jnp.exp(-jnp.abs(z)))
    return ls, ls - z


def _suffix_sums(x, upper):
    hi = x.astype(BF16)
    lo = (x - hi.astype(F32)).astype(BF16)
    return _dot(hi, upper) + _dot(lo, upper)


def _sb_prompt_kernel(q_ref, k_ref, v_ref, o_ref, kb_ref, vb_ref, *, tq):
    qi = pl.program_id(2)

    @pl.when(qi == 0)
    def _():
        kb_ref[...] = k_ref[...].astype(BF16)
        vb_ref[...] = v_ref[...].astype(BF16)

    q = q_ref[...].astype(F32) * (DH_SB ** -0.5)
    lane = lax.broadcasted_iota(jnp.int32, (tq, LANES), 1)
    row = lax.broadcasted_iota(jnp.int32, (tq, tq), 0)
    col = lax.broadcasted_iota(jnp.int32, (tq, tq), 1)
    upper = jnp.where(row > col, 1.0, 0.0).astype(BF16)
    causal = col < row

    outs = []
    for j in range(LANES // DH_SB):
        qm = jnp.where(lane // DH_SB == j, q, 0.0).astype(BF16)

        def block(kj, carry, acc, diagonal, qm=qm):
            start = pl.multiple_of(kj * tq, tq)
            z = _dot_nt(qm, kb_ref[pl.ds(start, tq), :])
            ls, l1m = _log_sigmoid_pair(z)
            if diagonal:
                l1m = jnp.where(causal, l1m, 0.0)
            a = jnp.exp(ls + _suffix_sums(l1m, upper) + carry)
            if diagonal:
                a = jnp.where(causal, a, 0.0)
            acc = acc + _dot(a.astype(BF16), vb_ref[pl.ds(start, tq), :])
            return carry + jnp.sum(l1m, axis=-1, keepdims=True), acc

        state = block(qi, jnp.zeros((tq, 1), F32), jnp.zeros((tq, LANES), F32), True)
        state = lax.fori_loop(
            0, qi, lambda i, s, block=block: block(qi - 1 - i, s[0], s[1], False), state)
        outs.append(state[1])
    o_ref[...] = jnp.where(lane < DH_SB, outs[0], outs[1]).astype(BF16)


def _sb_prompt(sq, sk, sv, *, tq):
    b, t, _ = sq.shape
    qspec = pl.BlockSpec((None, tq, LANES), lambda bi, hp, qi: (bi, qi, hp))
    kspec = pl.BlockSpec((None, t, LANES), lambda bi, hp, qi: (bi, 0, hp))
    return pl.pallas_call(
        functools.partial(_sb_prompt_kernel, tq=tq),
        out_shape=jax.ShapeDtypeStruct((b, t, BRANCH_W), BF16),
        grid=(b, BRANCH_W // LANES, t // tq),
        in_specs=[qspec, kspec, kspec],
        out_specs=qspec,
        scratch_shapes=[pltpu.VMEM((t, LANES), BF16), pltpu.VMEM((t, LANES), BF16)],
        compiler_params=_params(("parallel", "parallel", "arbitrary")),
        name="sb_prompt",
    )(sq, sk, sv)


def _diff_lambda(lv, lam_init):
    a = jnp.sum(lv[0:1] * lv[1:2], axis=-1, keepdims=True)
    b = jnp.sum(lv[2:3] * lv[3:4], axis=-1, keepdims=True)
    return jnp.exp(a) - jnp.exp(b) + lam_init


def _df_prompt_kernel(slope_ref, q_ref, k_ref, v_ref, lv_ref, nw_ref, o_ref, kb_ref, vb_ref,
                      *, tq, lam_init):
    h = pl.program_id(1)
    qi = pl.program_id(2)

    @pl.when(qi == 0)
    def _():
        kb_ref[...] = k_ref[...].astype(BF16)
        vb_ref[...] = v_ref[...].astype(BF16)

    slope = slope_ref[h]
    q = q_ref[...].astype(F32) * (DH_DF ** -0.5)
    lane = lax.broadcasted_iota(jnp.int32, (tq, LANES), 1)
    row = lax.broadcasted_iota(jnp.int32, (tq, tq), 0)
    col = lax.broadcasted_iota(jnp.int32, (tq, tq), 1)
    local_bias = slope * (row - col).astype(F32)
    causal = col <= row

    outs = []
    for n in range(2):
        qm = jnp.where(lane // DH_DF == n, q, 0.0).astype(BF16)

        def block(kj, m, l, acc, diagonal, qm=qm):
            start = pl.multiple_of(kj * tq, tq)
            s = _dot_nt(qm, kb_ref[pl.ds(start, tq), :]) - local_bias
            s = s - slope * ((qi - kj) * tq).astype(F32)
            if diagonal:
                s = jnp.where(causal, s, NEG_BIG)
            m_new = jnp.maximum(m, jnp.max(s, axis=-1, keepdims=True))
            p = jnp.exp(s - m_new)
            alpha = jnp.exp(m - m_new)
            l = alpha * l + jnp.sum(p, axis=-1, keepdims=True)
            acc = alpha * acc + _dot(p.astype(BF16), vb_ref[pl.ds(start, tq), :])
            return m_new, l, acc

        state = block(qi, jnp.full((tq, 1), NEG_BIG, F32), jnp.zeros((tq, 1), F32),
                      jnp.zeros((tq, LANES), F32), True)
        state = lax.fori_loop(
            0, qi, lambda i, s, block=block: block(qi - 1 - i, s[0], s[1], s[2], False), state)
        outs.append(state[2] / state[1])
    lam = _diff_lambda(lv_ref[...], lam_init)
    o = outs[0] - lam * outs[1]
    o_ref[...] = (_rms(o, nw_ref[...]) * (1.0 - lam_init)).astype(BF16)


def _df_prompt(dq, dk, dv, slopes, lv, nw, *, tq, lam_init):
    b, t, _ = dq.shape
    qspec = pl.BlockSpec((None, tq, LANES), lambda bi, h, qi: (bi, qi, h))
    kspec = pl.BlockSpec((None, t, LANES), lambda bi, h, qi: (bi, 0, h))
    return pl.pallas_call(
        functools.partial(_df_prompt_kernel, tq=tq, lam_init=lam_init),
        out_shape=jax.ShapeDtypeStruct((b, t, BRANCH_W), BF16),
        grid=(b, H_DF, t // tq),
        in_specs=[pl.BlockSpec(memory_space=pltpu.SMEM), qspec, kspec, kspec,
                  _resident(lv.shape), _resident(nw.shape)],
        out_specs=qspec,
        scratch_shapes=[pltpu.VMEM((t, LANES), BF16), pltpu.VMEM((t, LANES), BF16)],
        compiler_params=_params(("parallel", "parallel", "arbitrary")),
        name="df_prompt",
    )(slopes, dq, dk, dv, lv, nw)


def _gdn_gates(gab, prm):
    g = -jnp.exp(prm[0:1, :]) * _softplus(gab + prm[1:2, :])
    return g, _sigmoid(gab)


def _l2n(x):
    return x * lax.rsqrt(jnp.sum(x * x, axis=-1, keepdims=True) + L2_EPS)


def _gdn_out(o, nw, z):
    return (_rms(o, nw) * _silu(z)).astype(BF16)


def _gdn_prompt_kernel(xc_ref, gab_ref, gz_ref, cw_ref, prm_ref, nw_ref, cb_ref, s0_ref,
                       o_ref, sfin_ref, buf_ref, st_ref, *, nc):
    c = pl.program_id(1)
    ch = GDN_CHUNK

    @pl.when(c == 0)
    def _():
        buf_ref[0:8, :] = cb_ref[...]
        st_ref[...] = s0_ref[...]

    buf_ref[8:8 + ch, :] = xc_ref[...]
    xconv = None
    for j in range(CONV_W):
        t = cw_ref[j:j + 1, :] * buf_ref[5 + j:5 + j + ch, :]
        xconv = t if xconv is None else xconv + t
    buf_ref[0:8, :] = buf_ref[ch:ch + 8, :]
    xs = _silu(xconv)

    g_all, beta_all = _gdn_gates(gab_ref[...], prm_ref[...])
    row = lax.broadcasted_iota(jnp.int32, (ch, ch), 0)
    col = lax.broadcasted_iota(jnp.int32, (ch, ch), 1)
    tri = col <= row
    strict = col < row
    eye = jnp.where(row == col, 1.0, 0.0)
    cum_all = _dot(jnp.where(tri, 1.0, 0.0), g_all, _HI)
    cum_rows = cum_all.T

    hk = H_GD * DK_GD
    for h in range(H_GD):
        q = _l2n(xs[:, h * DK_GD:(h + 1) * DK_GD]) * (DK_GD ** -0.5)
        k = _l2n(xs[:, hk + h * DK_GD:hk + (h + 1) * DK_GD])
        v = xs[:, 2 * hk + h * DV_GD:2 * hk + (h + 1) * DV_GD]
        gc = cum_all[:, h:h + 1]
        gr = cum_rows[h:h + 1, :]
        beta = beta_all[:, H_GD + h:H_GD + h + 1]
        decay = jnp.exp(jnp.where(tri, gc - gr, NEG_BIG))
        eg = jnp.exp(gc)
        kb = k.astype(BF16)
        kk = _dot_nt(kb, kb)
        qk = _dot_nt(q.astype(BF16), kb) * decay
        pw = -jnp.where(strict, beta * decay * kk, 0.0)
        inv = eye + pw
        for _ in range(5):
            pw = _dot(pw, pw, _HI)
            inv = inv + _dot(inv, pw, _HI)
        rhs = jnp.concatenate([beta * v, (beta * eg) * k], axis=1)
        sol = _dot(inv, rhs, _HI)
        s = st_ref[h]
        sb = s.astype(BF16)
        u = sol[:, :DV_GD] - _dot(sol[:, DV_GD:].astype(BF16), sb)
        ub = u.astype(BF16)
        o = _dot((q * eg).astype(BF16), sb) + _dot(qk.astype(BF16), ub)
        g_last = gc[ch - 1:ch, :]
        st_ref[h] = jnp.exp(g_last) * s + _dot_tn((k * jnp.exp(g_last - gc)).astype(BF16), ub)
        o_ref[:, h * DV_GD:(h + 1) * DV_GD] = _gdn_out(
            o, nw_ref[...], gz_ref[:, h * DV_GD:(h + 1) * DV_GD])

    @pl.when(c == nc - 1)
    def _():
        sfin_ref[...] = st_ref[...]


def _gdn_prompt(xc, gab, gz, cw, prm, nw, cb8, s0):
    b, t, _ = xc.shape
    ch = GDN_CHUNK
    assert t % ch == 0
    nc = t // ch
    tok = lambda w: pl.BlockSpec((None, ch, w), lambda bi, c: (bi, c, 0))
    st = pl.BlockSpec((None, H_GD, DK_GD, DV_GD), lambda bi, c: (bi, 0, 0, 0))
    return pl.pallas_call(
        functools.partial(_gdn_prompt_kernel, nc=nc),
        out_shape=[jax.ShapeDtypeStruct((b, t, BRANCH_W), BF16),
                   jax.ShapeDtypeStruct((b, H_GD, DK_GD, DV_GD), F32)],
        grid=(b, nc),
        in_specs=[tok(GD_CONV_CH), tok(GAB_W), tok(BRANCH_W), _resident(cw.shape),
                  _resident(prm.shape), _resident(nw.shape),
                  pl.BlockSpec((None, 8, GD_CONV_CH), lambda bi, c: (bi, 0, 0)), st],
        out_specs=[tok(BRANCH_W), st],
        scratch_shapes=[pltpu.VMEM((ch + 8, GD_CONV_CH), F32),
                        pltpu.VMEM((H_GD, DK_GD, DV_GD), F32)],
        compiler_params=_params(("parallel", "arbitrary")),
        name="gdn_prompt",
    )(xc, gab, gz, cw, prm, nw, cb8, s0)


def _gdn_step_kernel(xc_ref, gab_ref, gz_ref, cw_ref, prm_ref, nw_ref, cb_ref, s0_ref,
                     o_ref, sn_ref):
    xconv = cw_ref[CONV_W - 1:CONV_W, :] * xc_ref[...]
    for j in range(CONV_W - 1):
        xconv = xconv + cw_ref[j:j + 1, :] * cb_ref[j:j + 1, :]
    xs = _silu(xconv)
    g_all, beta_all = _gdn_gates(gab_ref[...], prm_ref[...])
    first = lax.broadcasted_iota(jnp.int32, (8, DK_GD), 0) == 0
    rows8 = lambda x: jnp.where(first, jnp.broadcast_to(x, (8, x.shape[-1])), 0.0)
    hk = H_GD * DK_GD
    for h in range(H_GD):
        q = rows8(_l2n(xs[:, h * DK_GD:(h + 1) * DK_GD]) * (DK_GD ** -0.5))
        k = rows8(_l2n(xs[:, hk + h * DK_GD:hk + (h + 1) * DK_GD]))
        v = xs[:, 2 * hk + h * DV_GD:2 * hk + (h + 1) * DV_GD]
        eg = jnp.exp(g_all[:, h:h + 1])
        beta = beta_all[:, H_GD + h:H_GD + h + 1]
        s = s0_ref[h]
        u = beta * (v - eg * _dot(k, s, _HI))
        s_new = eg * s + _dot_tn(k, u, _HI)
        sn_ref[h] = s_new
        o = _dot(q, s_new, _HI)[0:1, :]
        o_ref[:, h * DV_GD:(h + 1) * DV_GD] = _gdn_out(
            o, nw_ref[...], gz_ref[:, h * DV_GD:(h + 1) * DV_GD])


def _gdn_step(xc, gab, gz, cw, prm, nw, cb, s0):
    b = xc.shape[0]
    tok = lambda w: pl.BlockSpec((None, 1, w), lambda bi: (bi, 0, 0))
    st = pl.BlockSpec((None, H_GD, DK_GD, DV_GD), lambda bi: (bi, 0, 0, 0))
    return pl.pallas_call(
        _gdn_step_kernel,
        out_shape=[jax.ShapeDtypeStruct((b, 1, BRANCH_W), BF16),
                   jax.ShapeDtypeStruct((b, H_GD, DK_GD, DV_GD), F32)],
        grid=(b,),
        in_specs=[tok(GD_CONV_CH), tok(GAB_W), tok(BRANCH_W), _resident(cw.shape),
                  _resident(prm.shape), _resident(nw.shape),
                  pl.BlockSpec((None, CONV_W - 1, GD_CONV_CH), lambda bi: (bi, 0, 0)), st],
        out_specs=[tok(BRANCH_W), st],
        compiler_params=_params(("parallel",)),
        name="gdn_step",
    )(xc, gab, gz, cw, prm, nw, cb, s0)


PAGES_PER_STEP = 4


def _page_specs(layer, n_pages, page_rows, width, descending):
    specs = []
    for i in range(PAGES_PER_STEP):
        def index(b, g, pt, i=i):
            p = g * PAGES_PER_STEP + i
            if descending:
                p = n_pages - 1 - p
            return (layer, pt[b, p], 0, 0)
        specs.append(pl.BlockSpec((None, None, page_rows, width), index))
    return specs


def _head_rows(q, n_rows, seg):
    w = q.shape[-1]
    r = lax.broadcasted_iota(jnp.int32, (n_rows, w), 0)
    c = lax.broadcasted_iota(jnp.int32, (n_rows, w), 1)
    return jnp.where(c // seg == r, jnp.broadcast_to(q, (n_rows, w)), 0.0)


def _sb_decode_kernel(pt_ref, q_ref, *refs, n_steps, page_rows):
    k_refs = refs[:PAGES_PER_STEP]
    v_refs = refs[PAGES_PER_STEP:2 * PAGES_PER_STEP]
    o_ref, carry_ref, acc_ref = refs[2 * PAGES_PER_STEP:]
    g = pl.program_id(1)

    @pl.when(g == 0)
    def _():
        carry_ref[...] = jnp.zeros(carry_ref.shape, F32)
        acc_ref[...] = jnp.zeros(acc_ref.shape, F32)

    qh = _head_rows(q_ref[...].astype(F32) * (DH_SB ** -0.5), H_SB, DH_SB).astype(BF16)
    row = lax.broadcasted_iota(jnp.int32, (page_rows, page_rows), 0)
    col = lax.broadcasted_iota(jnp.int32, (page_rows, page_rows), 1)
    upper = jnp.where(row > col, 1.0, 0.0).astype(BF16)
    carry = carry_ref[...]
    acc = acc_ref[...]
    for k_ref, v_ref in zip(k_refs, v_refs):
        z = _dot_nt(qh, k_ref[...].astype(BF16))
        ls, l1m = _log_sigmoid_pair(z)
        a = jnp.exp(ls + _suffix_sums(l1m, upper) + carry)
        acc = acc + _dot(a.astype(BF16), v_ref[...].astype(BF16))
        carry = carry + jnp.sum(l1m, axis=-1, keepdims=True)
    carry_ref[...] = carry
    acc_ref[...] = acc

    @pl.when(g == n_steps - 1)
    def _():
        own = _head_rows(jnp.ones((1, BRANCH_W), F32), H_SB, DH_SB)
        o_ref[...] = jnp.sum(own * acc, axis=0, keepdims=True).astype(BF16)


def _sb_decode(sq, cache_k, cache_v, page_table, layer):
    b = sq.shape[0]
    n_pages = page_table.shape[1]
    page_rows, width = cache_k.shape[2], cache_k.shape[3]
    assert n_pages % PAGES_PER_STEP == 0
    n_steps = n_pages // PAGES_PER_STEP
    tok = pl.BlockSpec((None, 1, BRANCH_W), lambda bi, g, pt: (bi, 0, 0))
    pages = _page_specs(layer, n_pages, page_rows, width, True)
    return pl.pallas_call(
        functools.partial(_sb_decode_kernel, n_steps=n_steps, page_rows=page_rows),
        out_shape=jax.ShapeDtypeStruct((b, 1, BRANCH_W), BF16),
        grid_spec=pltpu.PrefetchScalarGridSpec(
            num_scalar_prefetch=1, grid=(b, n_steps),
            in_specs=[tok] + pages + pages, out_specs=tok,
            scratch_shapes=[pltpu.VMEM((H_SB, 1), F32), pltpu.VMEM((H_SB, BRANCH_W), F32)]),
        compiler_params=_params(("parallel", "arbitrary")),
        name="sb_decode",
    )(page_table, sq, *([cache_k] * PAGES_PER_STEP), *([cache_v] * PAGES_PER_STEP))


def _df_decode_kernel(pt_ref, q_ref, kn_ref, vn_ref, slope_ref, lv_ref, nw_ref, *refs,
                      n_steps, page_rows, lam_init):
    k_refs = refs[:PAGES_PER_STEP]
    v_refs = refs[PAGES_PER_STEP:2 * PAGES_PER_STEP]
    o_ref, m_ref, l_ref, acc_ref = refs[2 * PAGES_PER_STEP:]
    g = pl.program_id(1)
    n_maps = 2 * H_DF
    past = n_steps * PAGES_PER_STEP * page_rows

    @pl.when(g == 0)
    def _():
        m_ref[...] = jnp.full(m_ref.shape, NEG_BIG, F32)
        l_ref[...] = jnp.zeros(l_ref.shape, F32)
        acc_ref[...] = jnp.zeros(acc_ref.shape, F32)

    qh = _head_rows(q_ref[...].astype(F32) * (DH_DF ** -0.5), n_maps, DH_DF)
    qb = qh.astype(BF16)
    slope = slope_ref[...][:, 0:1]
    within = lax.broadcasted_iota(jnp.int32, (n_maps, page_rows), 1)
    m, l, acc = m_ref[...], l_ref[...], acc_ref[...]

    def fold(s, vals, m, l, acc):
        m_new = jnp.maximum(m, jnp.max(s, axis=-1, keepdims=True))
        p = jnp.exp(s - m_new)
        alpha = jnp.exp(m - m_new)
        return m_new, alpha * l + jnp.sum(p, axis=-1, keepdims=True), alpha * acc + vals(p)

    for i, (k_ref, v_ref) in enumerate(zip(k_refs, v_refs)):
        first_key = (g * PAGES_PER_STEP + i) * page_rows
        dist = (past - first_key - within).astype(F32)
        s = _dot_nt(qb, k_ref[...].astype(BF16)) - slope * dist
        m, l, acc = fold(s, lambda p, v_ref=v_ref: _dot(p.astype(BF16), v_ref[...].astype(BF16)),
                         m, l, acc)
    m_ref[...], l_ref[...], acc_ref[...] = m, l, acc

    @pl.when(g == n_steps - 1)
    def _():
        s_own = jnp.sum(qh * kn_ref[...], axis=-1, keepdims=True)
        mf, lf, accf = fold(s_own, lambda p: p * vn_ref[...], m, l, acc)
        lam = _diff_lambda(lv_ref[...], lam_init)
        r = lax.broadcasted_iota(jnp.int32, (n_maps, BRANCH_W), 0)
        c = lax.broadcasted_iota(jnp.int32, (n_maps, BRANCH_W), 1)
        weight = jnp.where(r % 2 == 0, 1.0, -lam)
        own = jnp.where(c // DV_DF_W == r // 2, weight, 0.0)
        o = jnp.sum(own * (accf / lf), axis=0, keepdims=True)
        for h in range(H_DF):
            sl = slice(h * DV_DF_W, (h + 1) * DV_DF_W)
            o_ref[:, sl] = (_rms(o[:, sl], nw_ref[...]) * (1.0 - lam_init)).astype(BF16)


DV_DF_W = 2 * DH_DF


def _df_decode(dq, dk_new, dv_new, cache_k, cache_v, page_table, slopes8, lv, nw, layer,
               lam_init):
    b = dq.shape[0]
    n_pages = page_table.shape[1]
    page_rows, width = cache_k.shape[2], cache_k.shape[3]
    assert n_pages % PAGES_PER_STEP == 0
    n_steps = n_pages // PAGES_PER_STEP
    tok = pl.BlockSpec((None, 1, BRANCH_W), lambda bi, g, pt: (bi, 0, 0))
    const = lambda shape: pl.BlockSpec(shape, lambda bi, g, pt: (0,) * len(shape))
    pages = _page_specs(layer, n_pages, page_rows, width, False)
    n_maps = 2 * H_DF
    return pl.pallas_call(
        functools.partial(_df_decode_kernel, n_steps=n_steps, page_rows=page_rows,
                          lam_init=lam_init),
        out_shape=jax.ShapeDtypeStruct((b, 1, BRANCH_W), BF16),
        grid_spec=pltpu.PrefetchScalarGridSpec(
            num_scalar_prefetch=1, grid=(b, n_steps),
            in_specs=[tok, tok, tok, const(slopes8.shape), const(lv.shape), const(nw.shape)]
            + pages + pages,
            out_specs=tok,
            scratch_shapes=[pltpu.VMEM((n_maps, 1), F32), pltpu.VMEM((n_maps, 1), F32),
                            pltpu.VMEM((n_maps, BRANCH_W), F32)]),
        compiler_params=_params(("parallel", "arbitrary")),
        name="df_decode",
    )(page_table, dq, dk_new, dv_new, slopes8, lv, nw,
      *([cache_k] * PAGES_PER_STEP), *([cache_v] * PAGES_PER_STEP))


def _pack_w_in(w):
    a = 9 * 512
    gab = w[:, a:a + 2 * H_GD]
    gz = w[:, a + 2 * H_GD:a + 2 * H_GD + 512]
    gates = w[:, a + 2 * H_GD + 512:]
    pad = jnp.zeros((w.shape[0], GAB_W - 2 * H_GD), w.dtype)
    return jnp.concatenate([w[:, :a], gz, gates, gab, pad], axis=1).astype(BF16)


def _lane_row(v):
    return jnp.pad(v.astype(F32), (0, LANES - v.shape[0]))[None, :]


def _row_tile(m):
    return min(m, 256)


def kernel(x_prompt, x_sample, cache_sb_k, cache_sb_v, cache_df_k, cache_df_v, state_gdn,
           state_gdn_conv, page_table, norm_w, ffn_w_in, ffn_w_out, w_in, diff_lambda,
           diff_norm_w, gdn_conv_w, gdn_a_log, gdn_dt_bias, gdn_norm_w, w_branch, w_out,
           final_norm_w):
    depth = w_in.shape[0]
    bp, tp, _ = x_prompt.shape
    bs, ts, _ = x_sample.shape
    assert ts == 1
    n_pool, page = cache_sb_k.shape[1], cache_sb_k.shape[2]

    ffn_in = ffn_w_in.astype(BF16)
    ffn_out = ffn_w_out.astype(BF16)
    wb = w_branch.astype(BF16)
    wo = w_out.astype(BF16)
    fw = final_norm_w[None, :]
    slopes = 2.0 ** (-8.0 * jnp.arange(1, H_DF + 1, dtype=F32) / H_DF)
    slopes8 = jnp.broadcast_to(jnp.repeat(slopes, 2)[:, None], (2 * H_DF, LANES))
    ck = cache_sb_k.reshape(depth, n_pool, page, BRANCH_W)
    cv = cache_sb_v.reshape(depth, n_pool, page, BRANCH_W)
    dkc = cache_df_k.reshape(depth, n_pool, page, BRANCH_W)
    dvc = cache_df_v.reshape(depth, n_pool, page, BRANCH_W)

    xp = x_prompt.reshape(bp * tp, D_MODEL)
    xs = x_sample.reshape(bs, D_MODEL)
    tmp, tms = _row_tile(bp * tp), _row_tile(bs)
    p_states, s_states = [], []
    for l in range(depth):
        lam_init = 0.8 - 0.6 * math.exp(-0.3 * l)
        wl = _pack_w_in(w_in[l])
        nw = norm_w[l][:, None, :]
        prm = jnp.concatenate([_lane_row(gdn_a_log[l]), _lane_row(gdn_dt_bias[l]),
                               jnp.zeros((6, LANES), F32)], axis=0)
        gnw = gdn_norm_w[l][None, :]
        dnw = diff_norm_w[l][None, :]
        last = l == depth - 1

        xp = _ffn(xp, nw[0], ffn_in[l, 0], ffn_out[l, 0], fw, final=False, tm=tmp)
        sq, sk, sv, dq, dk, dv, xc, gz, gates, gab = _proj(xp, nw[1], wl, tm=tmp)
        r3 = lambda a: a.reshape(bp, tp, a.shape[-1])
        o_sb = _sb_prompt(r3(sq), r3(sk), r3(sv), tq=256)
        o_df = _df_prompt(r3(dq), r3(dk), r3(dv), slopes, diff_lambda[l], dnw, tq=256,
                          lam_init=lam_init)
        xc3 = r3(xc)
        o_gd, s_fin = _gdn_prompt(
            xc3, r3(gab), r3(gz), gdn_conv_w[l], prm, gnw,
            jnp.zeros((bp, 8, GD_CONV_CH), F32),
            jnp.zeros((bp, H_GD, DK_GD, DV_GD), F32))
        xp = _merge(xp, o_sb.reshape(bp * tp, BRANCH_W), o_df.reshape(bp * tp, BRANCH_W),
                    o_gd.reshape(bp * tp, BRANCH_W), gates, wb[l], wo[l], tm=tmp)
        xp = _ffn(xp, nw[2], ffn_in[l, 1], ffn_out[l, 1], fw, final=last, tm=tmp)
        p_states.append((
            sk.reshape(bp, tp, H_SB, DH_SB), sv.reshape(bp, tp, H_SB, DH_SB),
            dk.reshape(bp, tp, H_DF, 2, DH_DF), dv.reshape(bp, tp, H_DF, DV_DF_W),
            s_fin, xc3[:, tp - (CONV_W - 1):, :]))

        xs = _ffn(xs, nw[0], ffn_in[l, 0], ffn_out[l, 0], fw, final=False, tm=tms)
        sq, sk, sv, dq, dk, dv, xc, gz, gates, gab = _proj(xs, nw[1], wl, tm=tms)
        r3 = lambda a: a.reshape(bs, 1, a.shape[-1])
        o_sb = _sb_decode(r3(sq), ck, cv, page_table, l)
        o_df = _df_decode(r3(dq), r3(dk), r3(dv), dkc, dvc, page_table, slopes8,
                          diff_lambda[l], dnw, l, lam_init)
        o_gd, s_new = _gdn_step(r3(xc), r3(gab), r3(gz), gdn_conv_w[l], prm, gnw,
                                state_gdn_conv[l], state_gdn[l])
        xs = _merge(xs, o_sb.reshape(bs, BRANCH_W), o_df.reshape(bs, BRANCH_W),
                    o_gd.reshape(bs, BRANCH_W), gates, wb[l], wo[l], tm=tms)
        xs = _ffn(xs, nw[2], ffn_in[l, 1], ffn_out[l, 1], fw, final=last, tm=tms)
        new_buf = jnp.concatenate([state_gdn_conv[l][:, 1:, :], r3(xc)], axis=1)
        s_states.append((
            sk.reshape(bs, 1, H_SB, DH_SB), sv.reshape(bs, 1, H_SB, DH_SB),
            dk.reshape(bs, 1, H_DF, 2, DH_DF), dv.reshape(bs, 1, H_DF, DV_DF_W),
            s_new, new_buf))

    p_out = [jnp.stack(a) for a in zip(*p_states)]
    s_out = [jnp.stack(a) for a in zip(*s_states)]
    return (xp.reshape(bp, tp, D_MODEL), xs.reshape(bs, 1, D_MODEL), *p_out, *s_out)
```

```python
import functools
import math

import jax
import jax.numpy as jnp
from jax import lax
from jax.experimental import pallas as pl
from jax.experimental.pallas import tpu as pltpu

F32 = jnp.float32
BF16 = jnp.bfloat16

D_MODEL = 1024
DH_SB = 64
H_SB = 8
DH_DF = 64
H_DF = 4
DK_GD = 128
DV_GD = 128
H_GD = 4
CONV_W = 4
GD_CONV_CH = 2 * H_GD * DK_GD + H_GD * DV_GD
GDN_CHUNK = 64
BRANCH_W = 512
N_BRANCH = 3
D_FF = 2816
NORM_EPS = 1e-6
L2_EPS = 1e-6
NEG_BIG = -1e30

LANES = 128
FF_CHUNK = 256
GAB_W = LANES
VMEM_LIMIT = 56 * 1024 * 1024

PROJ_OUTS = (
    ("sq", 512, BF16), ("sk", 512, F32), ("sv", 512, F32),
    ("dq", 512, BF16), ("dk", 512, F32), ("dv", 512, F32),
    ("xc", GD_CONV_CH, F32), ("gz", 512, F32),
    ("gates", N_BRANCH * D_MODEL, F32), ("gab", GAB_W, F32),
)
PROJ_COLS = sum(w for _, w, _ in PROJ_OUTS)


def _dot(a, b, precision=None):
    return jnp.dot(a, b, preferred_element_type=F32, precision=precision)


def _dot_nt(a, b, precision=None):
    return lax.dot_general(a, b, (((1,), (1,)), ((), ())),
                           preferred_element_type=F32, precision=precision)


def _dot_tn(a, b, precision=None):
    return lax.dot_general(a, b, (((0,), (0,)), ((), ())),
                           preferred_element_type=F32, precision=precision)


_HI = lax.Precision.HIGHEST


def _rms(x, w):
    ms = jnp.mean(x * x, axis=-1, keepdims=True)
    return x * lax.rsqrt(ms + NORM_EPS) * w


def _sigmoid(x):
    return 1.0 / (1.0 + jnp.exp(-x))


def _silu(x):
    return x * _sigmoid(x)


def _softplus(x):
    return jnp.maximum(x, 0.0) + jnp.log(1.0 + jnp.exp(-jnp.abs(x)))


def _params(sem):
    return pltpu.CompilerParams(dimension_semantics=sem, vmem_limit_bytes=VMEM_LIMIT)


def _resident(shape):
    nd = len(shape)
    return pl.BlockSpec(shape, lambda *_: (0,) * nd, pipeline_mode=pl.Buffered(1))


def _ffn_kernel(x_ref, nw_ref, win_ref, wout_ref, fw_ref, o_ref, *, final):
    x = x_ref[...]
    h = _rms(x, nw_ref[...]).astype(BF16)
    acc = jnp.zeros(x.shape, F32)
    for c in range(D_FF // FF_CHUNK):
        lo = c * FF_CHUNK
        g = _dot(h, win_ref[:, lo:lo + FF_CHUNK])
        u = _dot(h, win_ref[:, D_FF + lo:D_FF + lo + FF_CHUNK])
        a = (_silu(g) * u).astype(BF16)
        acc = acc + _dot(a, wout_ref[lo:lo + FF_CHUNK, :])
    y = x + 0.5 * acc
    if final:
        y = _rms(y, fw_ref[...])
    o_ref[...] = y


def _ffn(x, nw, w_in, w_out, fw, *, final, tm):
    m = x.shape[0]
    row = pl.BlockSpec((tm, D_MODEL), lambda i: (i, 0))
    return pl.pallas_call(
        functools.partial(_ffn_kernel, final=final),
        out_shape=jax.ShapeDtypeStruct((m, D_MODEL), F32),
        grid=(m // tm,),
        in_specs=[row, _resident((1, D_MODEL)), _resident(w_in.shape),
                  _resident(w_out.shape), _resident((1, D_MODEL))],
        out_specs=row,
        compiler_params=_params(("parallel",)),
        name="ffn",
    )(x, nw, w_in, w_out, fw)


def _proj_kernel(x_ref, nw_ref, w_ref, *o_refs):
    h = _rms(x_ref[...], nw_ref[...]).astype(BF16)
    off = 0
    for (_, width, dt), o_ref in zip(PROJ_OUTS, o_refs):
        for lo in range(0, width, 512):
            n = min(512, width - lo)
            o_ref[:, lo:lo + n] = _dot(h, w_ref[:, off + lo:off + lo + n]).astype(dt)
        off += width


def _proj(x, nw, w, *, tm):
    m = x.shape[0]
    return pl.pallas_call(
        _proj_kernel,
        out_shape=[jax.ShapeDtypeStruct((m, wd), dt) for _, wd, dt in PROJ_OUTS],
        grid=(m // tm,),
        in_specs=[pl.BlockSpec((tm, D_MODEL), lambda i: (i, 0)),
                  _resident((1, D_MODEL)), _resident(w.shape)],
        out_specs=[pl.BlockSpec((tm, wd), lambda i: (i, 0)) for _, wd, _ in PROJ_OUTS],
        compiler_params=_params(("parallel",)),
        name="proj",
    )(x, nw, w)


def _merge_kernel(x_ref, osb_ref, odf_ref, ogd_ref, gates_ref, wb_ref, wo_ref, o_ref):
    m = None
    for i, b_ref in enumerate((osb_ref, odf_ref, ogd_ref)):
        p = _dot(b_ref[...], wb_ref[i])
        t = _sigmoid(gates_ref[:, i * D_MODEL:(i + 1) * D_MODEL]) * p
        m = t if m is None else m + t
    o_ref[...] = x_ref[...] + _dot(m.astype(BF16), wo_ref[...])


def _merge(x, o_sb, o_df, o_gd, gates, wb, wo, *, tm):
    m = x.shape[0]
    row = lambda w: pl.BlockSpec((tm, w), lambda i: (i, 0))
    return pl.pallas_call(
        _merge_kernel,
        out_shape=jax.ShapeDtypeStruct((m, D_MODEL), F32),
        grid=(m // tm,),
        in_specs=[row(D_MODEL), row(BRANCH_W), row(BRANCH_W), row(BRANCH_W),
                  row(N_BRANCH * D_MODEL), _resident(wb.shape), _resident(wo.shape)],
        out_specs=row(D_MODEL),
        compiler_params=_params(("parallel",)),
        name="merge",
    )(x, o_sb, o_df, o_gd, gates, wb, wo)


def _log_sigmoid_pair(z):
    ls = jnp.minimum(z, 0.0) - jnp.log(1.0 + jnp.exp(-jnp.abs(z)))
    return ls, ls - z


def _suffix_sums(x, upper):
    hi = x.astype(BF16)
    lo = (x - hi.astype(F32)).astype(BF16)
    return _dot(hi, upper) + _dot(lo, upper)


ATT_TK = 256
ATT_TQ = 512


def _stack_halves(q, seg):
    lane = lax.broadcasted_iota(jnp.int32, q.shape, 1)
    return jnp.concatenate(
        [jnp.where(lane // seg == j, q, 0.0) for j in range(LANES // seg)], axis=0).astype(BF16)


def _walk_key_blocks(qi, tq, block, state):
    nd = tq // ATT_TK
    for d in range(nd):
        state = block((qi + 1) * nd - 1 - d, state, True)
    n_free = qi * nd

    def two_blocks(i, st):
        kj = n_free - 1 - 2 * i
        return block(kj - 1, block(kj, st, False), False)

    state = lax.fori_loop(0, n_free // 2, two_blocks, state)
    return lax.fori_loop(0, n_free % 2, lambda i, st: block(0, st, False), state)


def _sb_prompt_kernel(q_ref, k_ref, v_ref, o_ref, kb_ref, vb_ref, *, tq):
    qi = pl.program_id(2)
    tk = ATT_TK

    @pl.when(qi == 0)
    def _():
        kb_ref[...] = k_ref[...].astype(BF16)
        vb_ref[...] = v_ref[...].astype(BF16)

    qs = _stack_halves(q_ref[...].astype(F32) * (DH_SB ** -0.5), DH_SB)
    rows = qs.shape[0]
    rowq = lax.broadcasted_iota(jnp.int32, (rows, tk), 0) % tq
    col = lax.broadcasted_iota(jnp.int32, (rows, tk), 1)
    ur = lax.broadcasted_iota(jnp.int32, (tk, tk), 0)
    uc = lax.broadcasted_iota(jnp.int32, (tk, tk), 1)
    upper = jnp.where(ur > uc, 1.0, 0.0).astype(BF16)

    def block(kj, state, masked):
        carry, acc = state
        start = pl.multiple_of(kj * tk, tk)
        z = _dot_nt(qs, kb_ref[pl.ds(start, tk), :])
        ls, l1m = _log_sigmoid_pair(z)
        if masked:
            before = col + (kj * tk - qi * tq) < rowq
            l1m = jnp.where(before, l1m, 0.0)
        a = jnp.exp(ls + _suffix_sums(l1m, upper) + carry)
        if masked:
            a = jnp.where(before, a, 0.0)
        acc = acc + _dot(a.astype(BF16), vb_ref[pl.ds(start, tk), :])
        return carry + jnp.sum(l1m, axis=-1, keepdims=True), acc

    _, acc = _walk_key_blocks(
        qi, tq, block, (jnp.zeros((rows, 1), F32), jnp.zeros((rows, LANES), F32)))
    lane = lax.broadcasted_iota(jnp.int32, (tq, LANES), 1)
    o_ref[...] = jnp.where(lane < DH_SB, acc[:tq], acc[tq:]).astype(BF16)


def _sb_prompt(sq, sk, sv, *, tq):
    b, t, _ = sq.shape
    assert tq % ATT_TK == 0 and t % tq == 0
    qspec = pl.BlockSpec((None, tq, LANES), lambda bi, hp, qi: (bi, qi, hp))
    kspec = pl.BlockSpec((None, t, LANES), lambda bi, hp, qi: (bi, 0, hp))
    return pl.pallas_call(
        functools.partial(_sb_prompt_kernel, tq=tq),
        out_shape=jax.ShapeDtypeStruct((b, t, BRANCH_W), BF16),
        grid=(b, BRANCH_W // LANES, t // tq),
        in_specs=[qspec, kspec, kspec],
        out_specs=qspec,
        scratch_shapes=[pltpu.VMEM((t, LANES), BF16), pltpu.VMEM((t, LANES), BF16)],
        compiler_params=_params(("parallel", "parallel", "arbitrary")),
        name="sb_prompt",
    )(sq, sk, sv)


def _diff_lambda(lv, lam_init):
    a = jnp.sum(lv[0:1] * lv[1:2], axis=-1, keepdims=True)
    b = jnp.sum(lv[2:3] * lv[3:4], axis=-1, keepdims=True)
    return jnp.exp(a) - jnp.exp(b) + lam_init


def _df_prompt_kernel(slope_ref, q_ref, k_ref, v_ref, lv_ref, nw_ref, o_ref, kb_ref, vb_ref,
                      *, tq, lam_init):
    h = pl.program_id(1)
    qi = pl.program_id(2)

    @pl.when(qi == 0)
    def _():
        kb_ref[...] = k_ref[...].astype(BF16)
        vb_ref[...] = v_ref[...].astype(BF16)

    tk = ATT_TK
    slope = slope_ref[h]
    qs = _stack_halves(q_ref[...].astype(F32) * (DH_DF ** -0.5), DH_DF)
    rows = qs.shape[0]
    rowq = lax.broadcasted_iota(jnp.int32, (rows, tk), 0) % tq
    col = lax.broadcasted_iota(jnp.int32, (rows, tk), 1)
    local_bias = slope * (rowq - col).astype(F32)

    def block(kj, state, masked):
        m, l, acc = state
        start = pl.multiple_of(kj * tk, tk)
        offset = qi * tq - kj * tk
        s = _dot_nt(qs, kb_ref[pl.ds(start, tk), :]) - local_bias - slope * offset.astype(F32)
        if masked:
            visible = col - offset <= rowq
            s = jnp.where(visible, s, NEG_BIG)
        m_new = jnp.maximum(m, jnp.max(s, axis=-1, keepdims=True))
        p = jnp.exp(s - m_new)
        if masked:
            p = jnp.where(visible, p, 0.0)
        alpha = jnp.exp(m - m_new)
        l = alpha * l + jnp.sum(p, axis=-1, keepdims=True)
        acc = alpha * acc + _dot(p.astype(BF16), vb_ref[pl.ds(start, tk), :])
        return m_new, l, acc

    _, l, acc = _walk_key_blocks(
        qi, tq, block, (jnp.full((rows, 1), NEG_BIG, F32), jnp.zeros((rows, 1), F32),
                        jnp.zeros((rows, LANES), F32)))
    lam = _diff_lambda(lv_ref[...], lam_init)
    o = acc[:tq] / l[:tq] - lam * (acc[tq:] / l[tq:])
    o_ref[...] = (_rms(o, nw_ref[...]) * (1.0 - lam_init)).astype(BF16)


def _df_prompt(dq, dk, dv, slopes, lv, nw, *, tq, lam_init):
    b, t, _ = dq.shape
    assert tq % ATT_TK == 0 and t % tq == 0
    qspec = pl.BlockSpec((None, tq, LANES), lambda bi, h, qi: (bi, qi, h))
    kspec = pl.BlockSpec((None, t, LANES), lambda bi, h, qi: (bi, 0, h))
    return pl.pallas_call(
        functools.partial(_df_prompt_kernel, tq=tq, lam_init=lam_init),
        out_shape=jax.ShapeDtypeStruct((b, t, BRANCH_W), BF16),
        grid=(b, H_DF, t // tq),
        in_specs=[pl.BlockSpec(memory_space=pltpu.SMEM), qspec, kspec, kspec,
                  _resident(lv.shape), _resident(nw.shape)],
        out_specs=qspec,
        scratch_shapes=[pltpu.VMEM((t, LANES), BF16), pltpu.VMEM((t, LANES), BF16)],
        compiler_params=_params(("parallel", "parallel", "arbitrary")),
        name="df_prompt",
    )(slopes, dq, dk, dv, lv, nw)


def _gdn_gates(gab, prm):
    g = -jnp.exp(prm[0:1, :]) * _softplus(gab + prm[1:2, :])
    return g, _sigmoid(gab)


def _l2n(x):
    return x * lax.rsqrt(jnp.sum(x * x, axis=-1, keepdims=True) + L2_EPS)


def _gdn_out(o, nw, z):
    return (_rms(o, nw) * _silu(z)).astype(BF16)


def _gdn_prompt_kernel(xc_ref, gab_ref, gz_ref, cw_ref, prm_ref, nw_ref, cb_ref, s0_ref,
                       o_ref, sfin_ref, buf_ref, st_ref, *, nc):
    c = pl.program_id(1)
    ch = GDN_CHUNK

    @pl.when(c == 0)
    def _():
        buf_ref[0:8, :] = cb_ref[...]
        st_ref[...] = s0_ref[...]

    buf_ref[8:8 + ch, :] = xc_ref[...]
    xconv = None
    for j in range(CONV_W):
        t = cw_ref[j:j + 1, :] * buf_ref[5 + j:5 + j + ch, :]
        xconv = t if xconv is None else xconv + t
    buf_ref[0:8, :] = buf_ref[ch:ch + 8, :]
    xs = _silu(xconv)

    g_all, beta_all = _gdn_gates(gab_ref[...], prm_ref[...])
    row = lax.broadcasted_iota(jnp.int32, (ch, ch), 0)
    col = lax.broadcasted_iota(jnp.int32, (ch, ch), 1)
    tri = col <= row
    strict = col < row
    eye = jnp.where(row == col, 1.0, 0.0)
    cum_all = _dot(jnp.where(tri, 1.0, 0.0), g_all, _HI)
    cum_rows = cum_all.T

    hk = H_GD * DK_GD
    for h in range(H_GD):
        q = _l2n(xs[:, h * DK_GD:(h + 1) * DK_GD]) * (DK_GD ** -0.5)
        k = _l2n(xs[:, hk + h * DK_GD:hk + (h + 1) * DK_GD])
        v = xs[:, 2 * hk + h * DV_GD:2 * hk + (h + 1) * DV_GD]
        gc = cum_all[:, h:h + 1]
        gr = cum_rows[h:h + 1, :]
        beta = beta_all[:, H_GD + h:H_GD + h + 1]
        decay = jnp.exp(jnp.where(tri, gc - gr, NEG_BIG))
        eg = jnp.exp(gc)
        kb = k.astype(BF16)
        kk = _dot_nt(kb, kb)
        qk = _dot_nt(q.astype(BF16), kb) * decay
        pw = -jnp.where(strict, beta * decay * kk, 0.0)
        inv = eye + pw
        for _ in range(5):
            pw = _dot(pw, pw, _HI)
            inv = inv + _dot(inv, pw, _HI)
        rhs = jnp.concatenate([beta * v, (beta * eg) * k], axis=1)
        sol = _dot(inv, rhs, _HI)
        s = st_ref[h]
        sb = s.astype(BF16)
        u = sol[:, :DV_GD] - _dot(sol[:, DV_GD:].astype(BF16), sb)
        ub = u.astype(BF16)
        o = _dot((q * eg).astype(BF16), sb) + _dot(qk.astype(BF16), ub)
        g_last = gc[ch - 1:ch, :]
        st_ref[h] = jnp.exp(g_last) * s + _dot_tn((k * jnp.exp(g_last - gc)).astype(BF16), ub)
        o_ref[:, h * DV_GD:(h + 1) * DV_GD] = _gdn_out(
            o, nw_ref[...], gz_ref[:, h * DV_GD:(h + 1) * DV_GD])

    @pl.when(c == nc - 1)
    def _():
        sfin_ref[...] = st_ref[...]


def _gdn_prompt(xc, gab, gz, cw, prm, nw, cb8, s0):
    b, t, _ = xc.shape
    ch = GDN_CHUNK
    assert t % ch == 0
    nc = t // ch
    tok = lambda w: pl.BlockSpec((None, ch, w), lambda bi, c: (bi, c, 0))
    st = pl.BlockSpec((None, H_GD, DK_GD, DV_GD), lambda bi, c: (bi, 0, 0, 0))
    return pl.pallas_call(
        functools.partial(_gdn_prompt_kernel, nc=nc),
        out_shape=[jax.ShapeDtypeStruct((b, t, BRANCH_W), BF16),
                   jax.ShapeDtypeStruct((b, H_GD, DK_GD, DV_GD), F32)],
        grid=(b, nc),
        in_specs=[tok(GD_CONV_CH), tok(GAB_W), tok(BRANCH_W), _resident(cw.shape),
                  _resident(prm.shape), _resident(nw.shape),
                  pl.BlockSpec((None, 8, GD_CONV_CH), lambda bi, c: (bi, 0, 0)), st],
        out_specs=[tok(BRANCH_W), st],
        scratch_shapes=[pltpu.VMEM((ch + 8, GD_CONV_CH), F32),
                        pltpu.VMEM((H_GD, DK_GD, DV_GD), F32)],
        compiler_params=_params(("parallel", "arbitrary")),
        name="gdn_prompt",
    )(xc, gab, gz, cw, prm, nw, cb8, s0)


def _gdn_step_kernel(xc_ref, gab_ref, gz_ref, cw_ref, prm_ref, nw_ref, cb_ref, s0_ref,
                     o_ref, sn_ref):
    xconv = cw_ref[CONV_W - 1:CONV_W, :] * xc_ref[...]
    for j in range(CONV_W - 1):
        xconv = xconv + cw_ref[j:j + 1, :] * cb_ref[j:j + 1, :]
    xs = _silu(xconv)
    g_all, beta_all = _gdn_gates(gab_ref[...], prm_ref[...])
    first = lax.broadcasted_iota(jnp.int32, (8, DK_GD), 0) == 0
    rows8 = lambda x: jnp.where(first, jnp.broadcast_to(x, (8, x.shape[-1])), 0.0)
    hk = H_GD * DK_GD
    for h in range(H_GD):
        q = rows8(_l2n(xs[:, h * DK_GD:(h + 1) * DK_GD]) * (DK_GD ** -0.5))
        k = rows8(_l2n(xs[:, hk + h * DK_GD:hk + (h + 1) * DK_GD]))
        v = xs[:, 2 * hk + h * DV_GD:2 * hk + (h + 1) * DV_GD]
        eg = jnp.exp(g_all[:, h:h + 1])
        beta = beta_all[:, H_GD + h:H_GD + h + 1]
        s = s0_ref[h]
        u = beta * (v - eg * _dot(k, s, _HI))
        s_new = eg * s + _dot_tn(k, u, _HI)
        sn_ref[h] = s_new
        o = _dot(q, s_new, _HI)[0:1, :]
        o_ref[:, h * DV_GD:(h + 1) * DV_GD] = _gdn_out(
            o, nw_ref[...], gz_ref[:, h * DV_GD:(h + 1) * DV_GD])


def _gdn_step(xc, gab, gz, cw, prm, nw, cb, s0):
    b = xc.shape[0]
    tok = lambda w: pl.BlockSpec((None, 1, w), lambda bi: (bi, 0, 0))
    st = pl.BlockSpec((None, H_GD, DK_GD, DV_GD), lambda bi: (bi, 0, 0, 0))
    return pl.pallas_call(
        _gdn_step_kernel,
        out_shape=[jax.ShapeDtypeStruct((b, 1, BRANCH_W), BF16),
                   jax.ShapeDtypeStruct((b, H_GD, DK_GD, DV_GD), F32)],
        grid=(b,),
        in_specs=[tok(GD_CONV_CH), tok(GAB_W), tok(BRANCH_W), _resident(cw.shape),
                  _resident(prm.shape), _resident(nw.shape),
                  pl.BlockSpec((None, CONV_W - 1, GD_CONV_CH), lambda bi: (bi, 0, 0)), st],
        out_specs=[tok(BRANCH_W), st],
        compiler_params=_params(("parallel",)),
        name="gdn_step",
    )(xc, gab, gz, cw, prm, nw, cb, s0)


PAGES_PER_STEP = 8


def _page_specs(layer, n_pages, block, descending):
    specs = []
    for i in range(PAGES_PER_STEP):
        def index(b, g, pt, i=i):
            p = g * PAGES_PER_STEP + i
            if descending:
                p = n_pages - 1 - p
            return (layer, pt[b, p]) + (0,) * len(block)
        specs.append(pl.BlockSpec((None, None) + block, index))
    return specs


def _store_lane_columns(q, qc_ref):
    n_seg, seg, lanes = qc_ref.shape
    first = lax.broadcasted_iota(jnp.int32, (8, seg), 0) == 0
    ones = jnp.where(lax.broadcasted_iota(jnp.int32, (8, lanes), 0) == 0, 1.0, 0.0).astype(BF16)
    for s in range(n_seg):
        qs = jnp.where(first, jnp.broadcast_to(q[:, s * seg:(s + 1) * seg], (8, seg)), 0.0)
        qc_ref[s] = _dot_tn(qs.astype(BF16), ones)


def _page_scores(k_ref, qc_ref):
    return jnp.concatenate(
        [jnp.sum(k_ref[s] * qc_ref[s], axis=0, keepdims=True) for s in range(k_ref.shape[0])],
        axis=0)


def _head_rows(q, n_rows, seg):
    w = q.shape[-1]
    r = lax.broadcasted_iota(jnp.int32, (n_rows, w), 0)
    c = lax.broadcasted_iota(jnp.int32, (n_rows, w), 1)
    return jnp.where(c // seg == r, jnp.broadcast_to(q, (n_rows, w)), 0.0)


def _sb_decode_kernel(pt_ref, q_ref, *refs, n_steps):
    k_refs = refs[:PAGES_PER_STEP]
    v_refs = refs[PAGES_PER_STEP:2 * PAGES_PER_STEP]
    o_ref, qc_ref, carry_ref, acc_ref = refs[2 * PAGES_PER_STEP:]
    g = pl.program_id(1)
    page_rows = qc_ref.shape[-1]

    @pl.when(g == 0)
    def _():
        _store_lane_columns(q_ref[...].astype(F32) * (DH_SB ** -0.5), qc_ref)
        carry_ref[...] = jnp.zeros(carry_ref.shape, F32)
        acc_ref[...] = jnp.zeros(acc_ref.shape, F32)

    row = lax.broadcasted_iota(jnp.int32, (page_rows, page_rows), 0)
    col = lax.broadcasted_iota(jnp.int32, (page_rows, page_rows), 1)
    upper = jnp.where(row > col, 1.0, 0.0).astype(BF16)
    z = jnp.concatenate([_page_scores(k_ref, qc_ref) for k_ref in k_refs], axis=0)
    ls, l1m = _log_sigmoid_pair(z)
    page_sums = jnp.sum(l1m, axis=-1, keepdims=True)
    carries = [carry_ref[...]]
    for i in range(PAGES_PER_STEP):
        carries.append(carries[-1] + page_sums[i * H_SB:(i + 1) * H_SB])
    carry_ref[...] = carries[-1]
    a = jnp.exp(ls + _suffix_sums(l1m, upper) + jnp.concatenate(carries[:-1], axis=0))
    for h in range(H_SB):
        t = acc_ref[h]
        for i, v_ref in enumerate(v_refs):
            t = t + v_ref[h] * a[i * H_SB + h:i * H_SB + h + 1, :]
        acc_ref[h] = t

    @pl.when(g == n_steps - 1)
    def _():
        ones = jnp.ones((8, page_rows), F32)
        o_ref[...] = jnp.concatenate(
            [_dot_nt(ones, acc_ref[h], _HI)[0:1, :] for h in range(H_SB)], axis=1).astype(BF16)


def _sb_decode(sq, cache_k, cache_v, page_table, layer):
    b = sq.shape[0]
    n_pages = page_table.shape[1]
    block = cache_k.shape[2:]
    assert n_pages % PAGES_PER_STEP == 0
    n_steps = n_pages // PAGES_PER_STEP
    tok = pl.BlockSpec((None, 1, BRANCH_W), lambda bi, g, pt: (bi, 0, 0))
    pages = _page_specs(layer, n_pages, block, True)
    return pl.pallas_call(
        functools.partial(_sb_decode_kernel, n_steps=n_steps),
        out_shape=jax.ShapeDtypeStruct((b, 1, BRANCH_W), BF16),
        grid_spec=pltpu.PrefetchScalarGridSpec(
            num_scalar_prefetch=1, grid=(b, n_steps),
            in_specs=[tok] + pages + pages, out_specs=tok,
            scratch_shapes=[pltpu.VMEM(block, F32), pltpu.VMEM((H_SB, 1), F32),
                            pltpu.VMEM(block, F32)]),
        compiler_params=_params(("parallel", "arbitrary")),
        name="sb_decode",
    )(page_table, sq, *([cache_k] * PAGES_PER_STEP), *([cache_v] * PAGES_PER_STEP))


def _df_decode_kernel(pt_ref, q_ref, kn_ref, vn_ref, slope_ref, lv_ref, nw_ref, *refs,
                      n_steps, lam_init):
    k_refs = refs[:PAGES_PER_STEP]
    v_refs = refs[PAGES_PER_STEP:2 * PAGES_PER_STEP]
    o_ref, qc_ref, m_ref, l_ref, acc_ref = refs[2 * PAGES_PER_STEP:]
    g = pl.program_id(1)
    n_maps = 2 * H_DF
    page_rows = qc_ref.shape[-1]
    past = n_steps * PAGES_PER_STEP * page_rows
    q = q_ref[...].astype(F32) * (DH_DF ** -0.5)

    @pl.when(g == 0)
    def _():
        _store_lane_columns(q, qc_ref)
        m_ref[...] = jnp.full(m_ref.shape, NEG_BIG, F32)
        l_ref[...] = jnp.zeros(l_ref.shape, F32)
        acc_ref[...] = jnp.zeros(acc_ref.shape, F32)

    rows = PAGES_PER_STEP * n_maps
    slope = jnp.concatenate([slope_ref[...][:, 0:1]] * PAGES_PER_STEP, axis=0)
    page_of_row = lax.broadcasted_iota(jnp.int32, (rows, page_rows), 0) // n_maps
    within = lax.broadcasted_iota(jnp.int32, (rows, page_rows), 1)
    first_key = (g * PAGES_PER_STEP + page_of_row) * page_rows
    dist = (past - first_key - within).astype(F32)
    flat = H_DF * page_rows
    ek = lax.broadcasted_iota(jnp.int32, (page_rows, flat), 0)
    er = lax.broadcasted_iota(jnp.int32, (page_rows, flat), 1)
    expand = jnp.where(er // H_DF == ek, 1.0, 0.0).astype(BF16)
    mr = lax.broadcasted_iota(jnp.int32, (rows, flat), 0)
    mc = lax.broadcasted_iota(jnp.int32, (rows, flat), 1)
    own_head = mc % H_DF == (mr % n_maps) // 2
    m, l, acc = m_ref[...], l_ref[...], acc_ref[...]

    def over_pages(op, x):
        return functools.reduce(op, [x[i * n_maps:(i + 1) * n_maps] for i in range(PAGES_PER_STEP)])

    s = jnp.concatenate([_page_scores(k_ref, qc_ref) for k_ref in k_refs], axis=0) - slope * dist
    m_new = jnp.maximum(m, over_pages(jnp.maximum, jnp.max(s, axis=-1, keepdims=True)))
    p = jnp.exp(s - jnp.concatenate([m_new] * PAGES_PER_STEP, axis=0))
    alpha = jnp.exp(m - m_new)
    pe = jnp.where(own_head, _dot(p.astype(BF16), expand), 0.0)
    acc = alpha * acc
    for i, v_ref in enumerate(v_refs):
        acc = acc + _dot(pe[i * n_maps:(i + 1) * n_maps].astype(BF16),
                         v_ref[...].reshape(flat, DV_DF_W).astype(BF16))
    m_ref[...] = m_new
    l_ref[...] = alpha * l + over_pages(jnp.add, jnp.sum(p, axis=-1, keepdims=True))
    acc_ref[...] = acc

    @pl.when(g == n_steps - 1)
    def _():
        s_own = jnp.sum(_head_rows(q, n_maps, DH_DF) * kn_ref[...], axis=-1, keepdims=True)
        v_own = jnp.concatenate(
            [vn_ref[:, (r // 2) * DV_DF_W:(r // 2 + 1) * DV_DF_W] for r in range(n_maps)], axis=0)
        m_last, l_last, acc_last = m_ref[...], l_ref[...], acc_ref[...]
        mf = jnp.maximum(m_last, s_own)
        p_own = jnp.exp(s_own - mf)
        scale_last = jnp.exp(m_last - mf)
        lf = scale_last * l_last + p_own
        accf = scale_last * acc_last + p_own * v_own
        lam = _diff_lambda(lv_ref[...], lam_init)
        o_map = accf / lf
        for h in range(H_DF):
            o = o_map[2 * h:2 * h + 1, :] - lam * o_map[2 * h + 1:2 * h + 2, :]
            o_ref[:, h * DV_DF_W:(h + 1) * DV_DF_W] = (
                _rms(o, nw_ref[...]) * (1.0 - lam_init)).astype(BF16)


DV_DF_W = 2 * DH_DF


def _df_decode(dq, dk_new, dv_new, cache_k, cache_v, page_table, slopes8, lv, nw, layer,
               lam_init):
    b = dq.shape[0]
    n_pages = page_table.shape[1]
    kblock, vblock = cache_k.shape[2:], cache_v.shape[2:]
    assert n_pages % PAGES_PER_STEP == 0
    n_steps = n_pages // PAGES_PER_STEP
    tok = pl.BlockSpec((None, 1, BRANCH_W), lambda bi, g, pt: (bi, 0, 0))
    const = lambda shape: pl.BlockSpec(shape, lambda bi, g, pt: (0,) * len(shape))
    n_maps = 2 * H_DF
    return pl.pallas_call(
        functools.partial(_df_decode_kernel, n_steps=n_steps, lam_init=lam_init),
        out_shape=jax.ShapeDtypeStruct((b, 1, BRANCH_W), BF16),
        grid_spec=pltpu.PrefetchScalarGridSpec(
            num_scalar_prefetch=1, grid=(b, n_steps),
            in_specs=[tok, tok, tok, const(slopes8.shape), const(lv.shape), const(nw.shape)]
            + _page_specs(layer, n_pages, kblock, False)
            + _page_specs(layer, n_pages, vblock, False),
            out_specs=tok,
            scratch_shapes=[pltpu.VMEM(kblock, F32), pltpu.VMEM((n_maps, 1), F32),
                            pltpu.VMEM((n_maps, 1), F32), pltpu.VMEM((n_maps, DV_DF_W), F32)]),
        compiler_params=_params(("parallel", "arbitrary")),
        name="df_decode",
    )(page_table, dq, dk_new, dv_new, slopes8, lv, nw,
      *([cache_k] * PAGES_PER_STEP), *([cache_v] * PAGES_PER_STEP))


def _cache_views(cache_sb_k, cache_sb_v, cache_df_k, cache_df_v):
    d, n, p = cache_sb_k.shape[:3]
    keys_last = lambda c: jnp.transpose(
        c.reshape(d, n, p, BRANCH_W // DH_SB, DH_SB), (0, 1, 3, 4, 2))
    return (keys_last(cache_sb_k), keys_last(cache_sb_v), keys_last(cache_df_k),
            cache_df_v.reshape(d, n, p * H_DF // 8, 8, DV_DF_W))


def _pack_w_in(w):
    a = 9 * 512
    gab = w[:, a:a + 2 * H_GD]
    gz = w[:, a + 2 * H_GD:a + 2 * H_GD + 512]
    gates = w[:, a + 2 * H_GD + 512:]
    pad = jnp.zeros((w.shape[0], GAB_W - 2 * H_GD), w.dtype)
    return jnp.concatenate([w[:, :a], gz, gates, gab, pad], axis=1).astype(BF16)


def _lane_row(v):
    return jnp.pad(v.astype(F32), (0, LANES - v.shape[0]))[None, :]


def _row_tile(m):
    return min(m, 256)


def kernel(x_prompt, x_sample, cache_sb_k, cache_sb_v, cache_df_k, cache_df_v, state_gdn,
           state_gdn_conv, page_table, norm_w, ffn_w_in, ffn_w_out, w_in, diff_lambda,
           diff_norm_w, gdn_conv_w, gdn_a_log, gdn_dt_bias, gdn_norm_w, w_branch, w_out,
           final_norm_w):
    depth = w_in.shape[0]
    bp, tp, _ = x_prompt.shape
    bs, ts, _ = x_sample.shape
    assert ts == 1

    ffn_in = ffn_w_in.astype(BF16)
    ffn_out = ffn_w_out.astype(BF16)
    wb = w_branch.astype(BF16)
    wo = w_out.astype(BF16)
    fw = final_norm_w[None, :]
    slopes = 2.0 ** (-8.0 * jnp.arange(1, H_DF + 1, dtype=F32) / H_DF)
    slopes8 = jnp.broadcast_to(jnp.repeat(slopes, 2)[:, None], (2 * H_DF, LANES))
    ck, cv, dkc, dvc = _cache_views(cache_sb_k, cache_sb_v, cache_df_k, cache_df_v)

    xp = x_prompt.reshape(bp * tp, D_MODEL)
    xs = x_sample.reshape(bs, D_MODEL)
    tmp, tms = _row_tile(bp * tp), _row_tile(bs)
    p_states, s_states = [], []
    for l in range(depth):
        lam_init = 0.8 - 0.6 * math.exp(-0.3 * l)
        wl = _pack_w_in(w_in[l])
        nw = norm_w[l][:, None, :]
        prm = jnp.concatenate([_lane_row(gdn_a_log[l]), _lane_row(gdn_dt_bias[l]),
                               jnp.zeros((6, LANES), F32)], axis=0)
        gnw = gdn_norm_w[l][None, :]
        dnw = diff_norm_w[l][None, :]
        last = l == depth - 1

        xp = _ffn(xp, nw[0], ffn_in[l, 0], ffn_out[l, 0], fw, final=False, tm=tmp)
        sq, sk, sv, dq, dk, dv, xc, gz, gates, gab = _proj(xp, nw[1], wl, tm=tmp)
        r3 = lambda a: a.reshape(bp, tp, a.shape[-1])
        o_sb = _sb_prompt(r3(sq), r3(sk), r3(sv), tq=ATT_TQ)
        o_df = _df_prompt(r3(dq), r3(dk), r3(dv), slopes, diff_lambda[l], dnw, tq=ATT_TQ,
                          lam_init=lam_init)
        xc3 = r3(xc)
        o_gd, s_fin = _gdn_prompt(
            xc3, r3(gab), r3(gz), gdn_conv_w[l], prm, gnw,
            jnp.zeros((bp, 8, GD_CONV_CH), F32),
            jnp.zeros((bp, H_GD, DK_GD, DV_GD), F32))
        xp = _merge(xp, o_sb.reshape(bp * tp, BRANCH_W), o_df.reshape(bp * tp, BRANCH_W),
                    o_gd.reshape(bp * tp, BRANCH_W), gates, wb[l], wo[l], tm=tmp)
        xp = _ffn(xp, nw[2], ffn_in[l, 1], ffn_out[l, 1], fw, final=last, tm=tmp)
        p_states.append((
            sk.reshape(bp, tp, H_SB, DH_SB), sv.reshape(bp, tp, H_SB, DH_SB),
            dk.reshape(bp, tp, H_DF, 2, DH_DF), dv.reshape(bp, tp, H_DF, DV_DF_W),
            s_fin, xc3[:, tp - (CONV_W - 1):, :]))

        xs = _ffn(xs, nw[0], ffn_in[l, 0], ffn_out[l, 0], fw, final=False, tm=tms)
        sq, sk, sv, dq, dk, dv, xc, gz, gates, gab = _proj(xs, nw[1], wl, tm=tms)
        r3 = lambda a: a.reshape(bs, 1, a.shape[-1])
        o_sb = _sb_decode(r3(sq), ck, cv, page_table, l)
        o_df = _df_decode(r3(dq), r3(dk), r3(dv), dkc, dvc, page_table, slopes8,
                          diff_lambda[l], dnw, l, lam_init)
        o_gd, s_new = _gdn_step(r3(xc), r3(gab), r3(gz), gdn_conv_w[l], prm, gnw,
                                state_gdn_conv[l], state_gdn[l])
        xs = _merge(xs, o_sb.reshape(bs, BRANCH_W), o_df.reshape(bs, BRANCH_W),
                    o_gd.reshape(bs, BRANCH_W), gates, wb[l], wo[l], tm=tms)
        xs = _ffn(xs, nw[2], ffn_in[l, 1], ffn_out[l, 1], fw, final=last, tm=tms)
        new_buf = jnp.concatenate([state_gdn_conv[l][:, 1:, :], r3(xc)], axis=1)
        s_states.append((
            sk.reshape(bs, 1, H_SB, DH_SB), sv.reshape(bs, 1, H_SB, DH_SB),
            dk.reshape(bs, 1, H_DF, 2, DH_DF), dv.reshape(bs, 1, H_DF, DV_DF_W),
            s_new, new_buf))

    p_out = [jnp.stack(a) for a in zip(*p_states)]
    s_out = [jnp.stack(a) for a in zip(*s_states)]
    return (xp.reshape(bp, tp, D_MODEL), xs.reshape(bs, 1, D_MODEL), *p_out, *s_out)
```

```python
import functools
import math

import jax
import jax.numpy as jnp
from jax import lax
from jax.experimental import pallas as pl
from jax.experimental.pallas import tpu as pltpu

F32 = jnp.float32
BF16 = jnp.bfloat16

D_MODEL = 1024
DH_SB = 64
H_SB = 8
DH_DF = 64
H_DF = 4
DK_GD = 128
DV_GD = 128
H_GD = 4
CONV_W = 4
GD_CONV_CH = 2 * H_GD * DK_GD + H_GD * DV_GD
GDN_CHUNK = 64
GDN_CHUNKS_PER_STEP = 4
BRANCH_W = 512
N_BRANCH = 3
D_FF = 2816
NORM_EPS = 1e-6
L2_EPS = 1e-6
NEG_BIG = -1e30

LANES = 128
FF_CHUNK = 256
GAB_W = LANES
VMEM_LIMIT = 56 * 1024 * 1024

PROJ_OUTS = (
    ("sq", 512, BF16), ("sk", 512, F32), ("sv", 512, F32),
    ("dq", 512, BF16), ("dk", 512, F32), ("dv", 512, F32),
    ("xc", GD_CONV_CH, F32), ("gz", 512, F32),
    ("gates", N_BRANCH * D_MODEL, F32), ("gab", GAB_W, F32),
)
PROJ_COLS = sum(w for _, w, _ in PROJ_OUTS)


def _dot(a, b, precision=None):
    return jnp.dot(a, b, preferred_element_type=F32, precision=precision)


def _dot_nt(a, b, precision=None):
    return lax.dot_general(a, b, (((1,), (1,)), ((), ())),
                           preferred_element_type=F32, precision=precision)


def _dot_tn(a, b, precision=None):
    return lax.dot_general(a, b, (((0,), (0,)), ((), ())),
                           preferred_element_type=F32, precision=precision)


_HI = lax.Precision.HIGHEST


def _rms(x, w):
    ms = jnp.mean(x * x, axis=-1, keepdims=True)
    return x * lax.rsqrt(ms + NORM_EPS) * w


def _sigmoid(x):
    return 1.0 / (1.0 + jnp.exp(-x))


def _silu(x):
    return x * _sigmoid(x)


def _softplus(x):
    return jnp.maximum(x, 0.0) + jnp.log(1.0 + jnp.exp(-jnp.abs(x)))


def _params(sem):
    return pltpu.CompilerParams(dimension_semantics=sem, vmem_limit_bytes=VMEM_LIMIT)


def _resident(shape):
    nd = len(shape)
    return pl.BlockSpec(shape, lambda *_: (0,) * nd, pipeline_mode=pl.Buffered(1))


def _ffn_kernel(x_ref, nw_ref, win_ref, wout_ref, fw_ref, o_ref, *, final):
    x = x_ref[...]
    h = _rms(x, nw_ref[...]).astype(BF16)
    acc = jnp.zeros(x.shape, F32)
    for c in range(D_FF // FF_CHUNK):
        lo = c * FF_CHUNK
        g = _dot(h, win_ref[:, lo:lo + FF_CHUNK])
        u = _dot(h, win_ref[:, D_FF + lo:D_FF + lo + FF_CHUNK])
        a = (_silu(g) * u).astype(BF16)
        acc = acc + _dot(a, wout_ref[lo:lo + FF_CHUNK, :])
    y = x + 0.5 * acc
    if final:
        y = _rms(y, fw_ref[...])
    o_ref[...] = y


def _ffn(x, nw, w_in, w_out, fw, *, final, tm):
    m = x.shape[0]
    row = pl.BlockSpec((tm, D_MODEL), lambda i: (i, 0))
    return pl.pallas_call(
        functools.partial(_ffn_kernel, final=final),
        out_shape=jax.ShapeDtypeStruct((m, D_MODEL), F32),
        grid=(m // tm,),
        in_specs=[row, _resident((1, D_MODEL)), _resident(w_in.shape),
                  _resident(w_out.shape), _resident((1, D_MODEL))],
        out_specs=row,
        compiler_params=_params(("parallel",)),
        name="ffn",
    )(x, nw, w_in, w_out, fw)


def _proj_kernel(x_ref, nw_ref, w_ref, *o_refs):
    h = _rms(x_ref[...], nw_ref[...]).astype(BF16)
    off = 0
    for (_, width, dt), o_ref in zip(PROJ_OUTS, o_refs):
        for lo in range(0, width, 512):
            n = min(512, width - lo)
            o_ref[:, lo:lo + n] = _dot(h, w_ref[:, off + lo:off + lo + n]).astype(dt)
        off += width


def _proj(x, nw, w, *, tm):
    m = x.shape[0]
    return pl.pallas_call(
        _proj_kernel,
        out_shape=[jax.ShapeDtypeStruct((m, wd), dt) for _, wd, dt in PROJ_OUTS],
        grid=(m // tm,),
        in_specs=[pl.BlockSpec((tm, D_MODEL), lambda i: (i, 0)),
                  _resident((1, D_MODEL)), _resident(w.shape)],
        out_specs=[pl.BlockSpec((tm, wd), lambda i: (i, 0)) for _, wd, _ in PROJ_OUTS],
        compiler_params=_params(("parallel",)),
        name="proj",
    )(x, nw, w)


def _merge_kernel(x_ref, osb_ref, odf_ref, ogd_ref, gates_ref, wb_ref, wo_ref, o_ref):
    m = None
    for i, b_ref in enumerate((osb_ref, odf_ref, ogd_ref)):
        p = _dot(b_ref[...], wb_ref[i])
        t = _sigmoid(gates_ref[:, i * D_MODEL:(i + 1) * D_MODEL]) * p
        m = t if m is None else m + t
    o_ref[...] = x_ref[...] + _dot(m.astype(BF16), wo_ref[...])


def _merge(x, o_sb, o_df, o_gd, gates, wb, wo, *, tm):
    m = x.shape[0]
    row = lambda w: pl.BlockSpec((tm, w), lambda i: (i, 0))
    return pl.pallas_call(
        _merge_kernel,
        out_shape=jax.ShapeDtypeStruct((m, D_MODEL), F32),
        grid=(m // tm,),
        in_specs=[row(D_MODEL), row(BRANCH_W), row(BRANCH_W), row(BRANCH_W),
                  row(N_BRANCH * D_MODEL), _resident(wb.shape), _resident(wo.shape)],
        out_specs=row(D_MODEL),
        compiler_params=_params(("parallel",)),
        name="merge",
    )(x, o_sb, o_df, o_gd, gates, wb, wo)


def _log_sigmoid_pair(z):
    ls = jnp.minimum(z, 0.0) - jnp.log(1.0 + jnp.exp(-jnp.abs(z)))
    return ls, ls - z


def _suffix_sums(x, upper):
    hi = x.astype(BF16)
    lo = (x - hi.astype(F32)).astype(BF16)
    return _dot(hi, upper) + _dot(lo, upper)


ATT_TK = 256
ATT_TQ = 512
SB_DONE = -110.0


def _stack_halves(q, seg):
    lane = lax.broadcasted_iota(jnp.int32, q.shape, 1)
    return jnp.concatenate(
        [jnp.where(lane // seg == j, q, 0.0) for j in range(LANES // seg)], axis=0).astype(BF16)


def _walk_key_blocks(qi, tq, blocks, state):
    nd = tq // ATT_TK
    state = blocks([(qi + 1) * nd - 1 - d for d in range(nd)], state, True)
    n_free = qi * nd

    def two_blocks(i, st):
        kj = n_free - 1 - 2 * i
        return blocks([kj, kj - 1], st, False)

    state = lax.fori_loop(0, n_free // 2, two_blocks, state)
    return lax.fori_loop(0, n_free % 2, lambda i, st: blocks([0], st, False), state)


def _sb_prompt_kernel(q_ref, k_ref, v_ref, o_ref, kb_ref, vb_ref, *, tq):
    qi = pl.program_id(2)
    tk = ATT_TK

    @pl.when(qi == 0)
    def _():
        kb_ref[...] = k_ref[...].astype(BF16)
        vb_ref[...] = v_ref[...].astype(BF16)

    qs = _stack_halves(q_ref[...].astype(F32) * (DH_SB ** -0.5), DH_SB)
    rows = qs.shape[0]
    rowq = lax.broadcasted_iota(jnp.int32, (rows, tk), 0) % tq
    col = lax.broadcasted_iota(jnp.int32, (rows, tk), 1)
    ur = lax.broadcasted_iota(jnp.int32, (tk, tk), 0)
    uc = lax.broadcasted_iota(jnp.int32, (tk, tk), 1)
    upper = jnp.where(ur > uc, 1.0, 0.0).astype(BF16)

    def blocks(kjs, state, masked):
        carry, acc = state
        starts = [pl.multiple_of(kj * tk, tk) for kj in kjs]
        pairs = [_log_sigmoid_pair(_dot_nt(qs, kb_ref[pl.ds(s, tk), :])) for s in starts]
        lss = [p[0] for p in pairs]
        l1ms = [p[1] for p in pairs]
        if masked:
            befores = [col + (kj * tk - qi * tq) < rowq for kj in kjs]
            l1ms = [jnp.where(b, x, 0.0) for b, x in zip(befores, l1ms)]
        sufs = [_suffix_sums(x, upper) for x in l1ms]
        carries = [carry]
        for x in l1ms:
            carries.append(carries[-1] + jnp.sum(x, axis=-1, keepdims=True))
        probs = [jnp.exp(ls + suf + c) for ls, suf, c in zip(lss, sufs, carries)]
        if masked:
            probs = [jnp.where(b, a, 0.0) for b, a in zip(befores, probs)]
        for a, s in zip(probs, starts):
            acc = acc + _dot(a.astype(BF16), vb_ref[pl.ds(s, tk), :])
        return carries[-1], acc

    nd = tq // tk
    state = blocks([(qi + 1) * nd - 1 - d for d in range(nd)],
                   (jnp.zeros((rows, 1), F32), jnp.zeros((rows, LANES), F32)), True)

    def unfinished(loop):
        kj, carry, _ = loop
        return jnp.logical_and(kj >= 0, jnp.max(carry) > SB_DONE)

    def one_block(loop):
        kj, carry, acc = loop
        carry, acc = blocks([kj], (carry, acc), False)
        return kj - 1, carry, acc

    _, _, acc = lax.while_loop(unfinished, one_block, (qi * nd - 1,) + state)
    lane = lax.broadcasted_iota(jnp.int32, (tq, LANES), 1)
    o_ref[...] = jnp.where(lane < DH_SB, acc[:tq], acc[tq:]).astype(BF16)


def _sb_prompt(sq, sk, sv, *, tq):
    b, t, _ = sq.shape
    assert tq % ATT_TK == 0 and t % tq == 0
    qspec = pl.BlockSpec((None, tq, LANES), lambda bi, hp, qi: (bi, qi, hp))
    kspec = pl.BlockSpec((None, t, LANES), lambda bi, hp, qi: (bi, 0, hp))
    return pl.pallas_call(
        functools.partial(_sb_prompt_kernel, tq=tq),
        out_shape=jax.ShapeDtypeStruct((b, t, BRANCH_W), BF16),
        grid=(b, BRANCH_W // LANES, t // tq),
        in_specs=[qspec, kspec, kspec],
        out_specs=qspec,
        scratch_shapes=[pltpu.VMEM((t, LANES), BF16), pltpu.VMEM((t, LANES), BF16)],
        compiler_params=_params(("parallel", "parallel", "arbitrary")),
        name="sb_prompt",
    )(sq, sk, sv)


def _diff_lambda(lv, lam_init):
    a = jnp.sum(lv[0:1] * lv[1:2], axis=-1, keepdims=True)
    b = jnp.sum(lv[2:3] * lv[3:4], axis=-1, keepdims=True)
    return jnp.exp(a) - jnp.exp(b) + lam_init


def _df_prompt_kernel(slope_ref, q_ref, k_ref, v_ref, lv_ref, nw_ref, o_ref, kb_ref, vb_ref,
                      *, tq, lam_init):
    h = pl.program_id(1)
    qi = pl.program_id(2)

    @pl.when(qi == 0)
    def _():
        kb_ref[...] = k_ref[...].astype(BF16)
        vb_ref[...] = v_ref[...].astype(BF16)

    tk = ATT_TK
    slope = slope_ref[h]
    qs = _stack_halves(q_ref[...].astype(F32) * (DH_DF ** -0.5), DH_DF)
    rows = qs.shape[0]
    rowq = lax.broadcasted_iota(jnp.int32, (rows, tk), 0) % tq
    col = lax.broadcasted_iota(jnp.int32, (rows, tk), 1)
    local_bias = slope * (rowq - col).astype(F32)

    def block(kj, state, masked):
        m, l, acc = state
        start = pl.multiple_of(kj * tk, tk)
        offset = qi * tq - kj * tk
        s = _dot_nt(qs, kb_ref[pl.ds(start, tk), :]) - local_bias - slope * offset.astype(F32)
        if masked:
            visible = col - offset <= rowq
            s = jnp.where(visible, s, NEG_BIG)
        m_new = jnp.maximum(m, jnp.max(s, axis=-1, keepdims=True))
        p = jnp.exp(s - m_new)
        if masked:
            p = jnp.where(visible, p, 0.0)
        alpha = jnp.exp(m - m_new)
        l = alpha * l + jnp.sum(p, axis=-1, keepdims=True)
        acc = alpha * acc + _dot(p.astype(BF16), vb_ref[pl.ds(start, tk), :])
        return m_new, l, acc

    def blocks(kjs, state, masked):
        for kj in kjs:
            state = block(kj, state, masked)
        return state

    _, l, acc = _walk_key_blocks(
        qi, tq, blocks, (jnp.full((rows, 1), NEG_BIG, F32), jnp.zeros((rows, 1), F32),
                         jnp.zeros((rows, LANES), F32)))
    lam = _diff_lambda(lv_ref[...], lam_init)
    o = acc[:tq] / l[:tq] - lam * (acc[tq:] / l[tq:])
    o_ref[...] = (_rms(o, nw_ref[...]) * (1.0 - lam_init)).astype(BF16)


def _df_prompt(dq, dk, dv, slopes, lv, nw, *, tq, lam_init):
    b, t, _ = dq.shape
    assert tq % ATT_TK == 0 and t % tq == 0
    qspec = pl.BlockSpec((None, tq, LANES), lambda bi, h, qi: (bi, qi, h))
    kspec = pl.BlockSpec((None, t, LANES), lambda bi, h, qi: (bi, 0, h))
    return pl.pallas_call(
        functools.partial(_df_prompt_kernel, tq=tq, lam_init=lam_init),
        out_shape=jax.ShapeDtypeStruct((b, t, BRANCH_W), BF16),
        grid=(b, H_DF, t // tq),
        in_specs=[pl.BlockSpec(memory_space=pltpu.SMEM), qspec, kspec, kspec,
                  _resident(lv.shape), _resident(nw.shape)],
        out_specs=qspec,
        scratch_shapes=[pltpu.VMEM((t, LANES), BF16), pltpu.VMEM((t, LANES), BF16)],
        compiler_params=_params(("parallel", "parallel", "arbitrary")),
        name="df_prompt",
    )(slopes, dq, dk, dv, lv, nw)


def _gdn_gates(gab, prm):
    g = -jnp.exp(prm[0:1, :]) * _softplus(gab + prm[1:2, :])
    return g, _sigmoid(gab)


def _l2n(x):
    return x * lax.rsqrt(jnp.sum(x * x, axis=-1, keepdims=True) + L2_EPS)


def _two_terms(x):
    hi = x.astype(BF16)
    return hi, (x - hi.astype(F32)).astype(BF16)


def _split_lhs(a):
    hi, lo = _two_terms(a)
    return jnp.concatenate([hi, hi, lo], axis=1)


def _split_rhs(b):
    hi, lo = _two_terms(b)
    return jnp.concatenate([hi, lo, hi], axis=0)


def _gdn_out(o, nw, z):
    return (_rms(o, nw) * _silu(z)).astype(BF16)


def _gdn_prompt_kernel(xc_ref, gab_ref, gz_ref, cw_ref, prm_ref, nw_ref, cb_ref, s0_ref,
                       o_ref, sfin_ref, buf_ref, st_ref, *, n_steps):
    c = pl.program_id(1)
    ch = GDN_CHUNK
    rows = GDN_CHUNKS_PER_STEP * ch

    @pl.when(c == 0)
    def _():
        buf_ref[0:8, :] = cb_ref[...]
        st_ref[...] = s0_ref[...]

    buf_ref[8:8 + rows, :] = xc_ref[...]
    xconv = None
    for j in range(CONV_W):
        t = cw_ref[j:j + 1, :] * buf_ref[5 + j:5 + j + rows, :]
        xconv = t if xconv is None else xconv + t
    buf_ref[0:8, :] = buf_ref[rows:rows + 8, :]
    xs = _silu(xconv)

    g_all, beta_all = _gdn_gates(gab_ref[...], prm_ref[...])
    row = lax.broadcasted_iota(jnp.int32, (ch, ch), 0)
    col = lax.broadcasted_iota(jnp.int32, (ch, ch), 1)
    tri = col <= row
    strict = col < row
    eye = jnp.where(row == col, 1.0, 0.0)
    lower = jnp.where(tri, 1.0, 0.0)
    hk = H_GD * DK_GD

    problems = [(j, h) for j in range(GDN_CHUNKS_PER_STEP) for h in range(H_GD)]
    cums = []
    for j in range(GDN_CHUNKS_PER_STEP):
        cum_all = _dot(lower, g_all[j * ch:(j + 1) * ch], _HI)
        cums.append((cum_all, cum_all.T))

    def first_stage(j, h):
        r0 = j * ch
        q = _l2n(xs[r0:r0 + ch, h * DK_GD:(h + 1) * DK_GD]) * (DK_GD ** -0.5)
        k = _l2n(xs[r0:r0 + ch, hk + h * DK_GD:hk + (h + 1) * DK_GD])
        v = xs[r0:r0 + ch, 2 * hk + h * DV_GD:2 * hk + (h + 1) * DV_GD]
        gc = cums[j][0][:, h:h + 1]
        gr = cums[j][1][h:h + 1, :]
        beta = beta_all[r0:r0 + ch, H_GD + h:H_GD + h + 1]
        decay = jnp.exp(jnp.where(tri, gc - gr, NEG_BIG))
        eg = jnp.exp(gc)
        g_last = gc[ch - 1:ch, :]
        kb = k.astype(BF16)
        qk = (_dot_nt(q.astype(BF16), kb) * decay).astype(BF16)
        pw = -jnp.where(strict, beta * decay * _dot_nt(kb, kb), 0.0)
        rhs = _split_rhs(jnp.concatenate([beta * v, (beta * eg) * k], axis=1))
        carry_over = ((q * eg).astype(BF16), qk, (k * jnp.exp(g_last - gc)).astype(BF16),
                      jnp.exp(g_last))
        return pw, rhs, carry_over

    staged = [first_stage(j, h) for j, h in problems]
    pws = [p[0] for p in staged]
    invs = [eye + pw for pw in pws]
    for _ in range(5):
        pws = [_dot(_split_lhs(pw), _split_rhs(pw)) for pw in pws]
        invs = [inv + _dot(_split_lhs(inv), _split_rhs(pw)) for inv, pw in zip(invs, pws)]
    sols = [_dot(_split_lhs(inv), p[1]) for inv, p in zip(invs, staged)]
    prepared = {jh: (sol[:, :DV_GD], sol[:, DV_GD:].astype(BF16)) + p[2]
                for jh, sol, p in zip(problems, sols, staged)}

    states = [st_ref[h] for h in range(H_GD)]
    for j in range(GDN_CHUNKS_PER_STEP):
        for h in range(H_GD):
            sol_v, sol_k, q_eg, qk, k_dec, e_last = prepared[(j, h)]
            s = states[h]
            sb = s.astype(BF16)
            ub = (sol_v - _dot(sol_k, sb)).astype(BF16)
            o = _dot(q_eg, sb) + _dot(qk, ub)
            states[h] = e_last * s + _dot_tn(k_dec, ub)
            o_ref[j * ch:(j + 1) * ch, h * DV_GD:(h + 1) * DV_GD] = _gdn_out(
                o, nw_ref[...], gz_ref[j * ch:(j + 1) * ch, h * DV_GD:(h + 1) * DV_GD])
    for h in range(H_GD):
        st_ref[h] = states[h]

    @pl.when(c == n_steps - 1)
    def _():
        sfin_ref[...] = st_ref[...]


def _gdn_prompt(xc, gab, gz, cw, prm, nw, cb8, s0):
    b, t, _ = xc.shape
    rows = GDN_CHUNKS_PER_STEP * GDN_CHUNK
    assert t % rows == 0
    n_steps = t // rows
    tok = lambda w: pl.BlockSpec((None, rows, w), lambda bi, c: (bi, c, 0))
    st = pl.BlockSpec((None, H_GD, DK_GD, DV_GD), lambda bi, c: (bi, 0, 0, 0))
    return pl.pallas_call(
        functools.partial(_gdn_prompt_kernel, n_steps=n_steps),
        out_shape=[jax.ShapeDtypeStruct((b, t, BRANCH_W), BF16),
                   jax.ShapeDtypeStruct((b, H_GD, DK_GD, DV_GD), F32)],
        grid=(b, n_steps),
        in_specs=[tok(GD_CONV_CH), tok(GAB_W), tok(BRANCH_W), _resident(cw.shape),
                  _resident(prm.shape), _resident(nw.shape),
                  pl.BlockSpec((None, 8, GD_CONV_CH), lambda bi, c: (bi, 0, 0)), st],
        out_specs=[tok(BRANCH_W), st],
        scratch_shapes=[pltpu.VMEM((rows + 8, GD_CONV_CH), F32),
                        pltpu.VMEM((H_GD, DK_GD, DV_GD), F32)],
        compiler_params=_params(("parallel", "arbitrary")),
        name="gdn_prompt",
    )(xc, gab, gz, cw, prm, nw, cb8, s0)


def _gdn_step_kernel(xc_ref, gab_ref, gz_ref, cw_ref, prm_ref, nw_ref, cb_ref, s0_ref,
                     o_ref, sn_ref):
    xconv = cw_ref[CONV_W - 1:CONV_W, :] * xc_ref[...]
    for j in range(CONV_W - 1):
        xconv = xconv + cw_ref[j:j + 1, :] * cb_ref[j:j + 1, :]
    xs = _silu(xconv)
    g_all, beta_all = _gdn_gates(gab_ref[...], prm_ref[...])
    first = lax.broadcasted_iota(jnp.int32, (8, DK_GD), 0) == 0
    rows8 = lambda x: jnp.where(first, jnp.broadcast_to(x, (8, x.shape[-1])), 0.0)
    hk = H_GD * DK_GD
    for h in range(H_GD):
        q = rows8(_l2n(xs[:, h * DK_GD:(h + 1) * DK_GD]) * (DK_GD ** -0.5))
        k = rows8(_l2n(xs[:, hk + h * DK_GD:hk + (h + 1) * DK_GD]))
        v = xs[:, 2 * hk + h * DV_GD:2 * hk + (h + 1) * DV_GD]
        eg = jnp.exp(g_all[:, h:h + 1])
        beta = beta_all[:, H_GD + h:H_GD + h + 1]
        s = s0_ref[h]
        u = beta * (v - eg * _dot(k, s, _HI))
        s_new = eg * s + _dot_tn(k, u, _HI)
        sn_ref[h] = s_new
        o = _dot(q, s_new, _HI)[0:1, :]
        o_ref[:, h * DV_GD:(h + 1) * DV_GD] = _gdn_out(
            o, nw_ref[...], gz_ref[:, h * DV_GD:(h + 1) * DV_GD])


def _gdn_step(xc, gab, gz, cw, prm, nw, cb, s0):
    b = xc.shape[0]
    tok = lambda w: pl.BlockSpec((None, 1, w), lambda bi: (bi, 0, 0))
    st = pl.BlockSpec((None, H_GD, DK_GD, DV_GD), lambda bi: (bi, 0, 0, 0))
    return pl.pallas_call(
        _gdn_step_kernel,
        out_shape=[jax.ShapeDtypeStruct((b, 1, BRANCH_W), BF16),
                   jax.ShapeDtypeStruct((b, H_GD, DK_GD, DV_GD), F32)],
        grid=(b,),
        in_specs=[tok(GD_CONV_CH), tok(GAB_W), tok(BRANCH_W), _resident(cw.shape),
                  _resident(prm.shape), _resident(nw.shape),
                  pl.BlockSpec((None, CONV_W - 1, GD_CONV_CH), lambda bi: (bi, 0, 0)), st],
        out_specs=[tok(BRANCH_W), st],
        compiler_params=_params(("parallel",)),
        name="gdn_step",
    )(xc, gab, gz, cw, prm, nw, cb, s0)


PAGES_PER_STEP = 8


def _page_specs(layer, n_pages, block, descending):
    specs = []
    for i in range(PAGES_PER_STEP):
        def index(b, g, pt, i=i):
            p = g * PAGES_PER_STEP + i
            if descending:
                p = n_pages - 1 - p
            return (layer, pt[b, p]) + (0,) * len(block)
        specs.append(pl.BlockSpec((None, None) + block, index))
    return specs


def _store_lane_columns(q, qc_ref):
    n_seg, seg, lanes = qc_ref.shape
    first = lax.broadcasted_iota(jnp.int32, (8, seg), 0) == 0
    ones = jnp.where(lax.broadcasted_iota(jnp.int32, (8, lanes), 0) == 0, 1.0, 0.0).astype(BF16)
    for s in range(n_seg):
        qs = jnp.where(first, jnp.broadcast_to(q[:, s * seg:(s + 1) * seg], (8, seg)), 0.0)
        qc_ref[s] = _dot_tn(qs.astype(BF16), ones)


def _page_scores(k_ref, qc_ref):
    return jnp.concatenate(
        [jnp.sum(k_ref[s] * qc_ref[s], axis=0, keepdims=True) for s in range(k_ref.shape[0])],
        axis=0)


def _head_rows(q, n_rows, seg):
    w = q.shape[-1]
    r = lax.broadcasted_iota(jnp.int32, (n_rows, w), 0)
    c = lax.broadcasted_iota(jnp.int32, (n_rows, w), 1)
    return jnp.where(c // seg == r, jnp.broadcast_to(q, (n_rows, w)), 0.0)


def _sb_decode_kernel(pt_ref, q_ref, *refs, n_steps):
    k_refs = refs[:PAGES_PER_STEP]
    v_refs = refs[PAGES_PER_STEP:2 * PAGES_PER_STEP]
    o_ref, qc_ref, carry_ref, acc_ref = refs[2 * PAGES_PER_STEP:]
    g = pl.program_id(1)
    page_rows = qc_ref.shape[-1]

    @pl.when(g == 0)
    def _():
        _store_lane_columns(q_ref[...].astype(F32) * (DH_SB ** -0.5), qc_ref)
        carry_ref[...] = jnp.zeros(carry_ref.shape, F32)
        acc_ref[...] = jnp.zeros(acc_ref.shape, F32)

    row = lax.broadcasted_iota(jnp.int32, (page_rows, page_rows), 0)
    col = lax.broadcasted_iota(jnp.int32, (page_rows, page_rows), 1)
    upper = jnp.where(row > col, 1.0, 0.0).astype(BF16)
    z = jnp.concatenate([_page_scores(k_ref, qc_ref) for k_ref in k_refs], axis=0)
    ls, l1m = _log_sigmoid_pair(z)
    page_sums = jnp.sum(l1m, axis=-1, keepdims=True)
    carries = [carry_ref[...]]
    for i in range(PAGES_PER_STEP):
        carries.append(carries[-1] + page_sums[i * H_SB:(i + 1) * H_SB])
    carry_ref[...] = carries[-1]
    a = jnp.exp(ls + _suffix_sums(l1m, upper) + jnp.concatenate(carries[:-1], axis=0))
    for h in range(H_SB):
        t = acc_ref[h]
        for i, v_ref in enumerate(v_refs):
            t = t + v_ref[h] * a[i * H_SB + h:i * H_SB + h + 1, :]
        acc_ref[h] = t

    @pl.when(g == n_steps - 1)
    def _():
        ones = jnp.ones((8, page_rows), F32)
        o_ref[...] = jnp.concatenate(
            [_dot_nt(ones, acc_ref[h], _HI)[0:1, :] for h in range(H_SB)], axis=1).astype(BF16)


def _sb_decode(sq, cache_k, cache_v, page_table, layer):
    b = sq.shape[0]
    n_pages = page_table.shape[1]
    block = cache_k.shape[2:]
    assert n_pages % PAGES_PER_STEP == 0
    n_steps = n_pages // PAGES_PER_STEP
    tok = pl.BlockSpec((None, 1, BRANCH_W), lambda bi, g, pt: (bi, 0, 0))
    pages = _page_specs(layer, n_pages, block, True)
    return pl.pallas_call(
        functools.partial(_sb_decode_kernel, n_steps=n_steps),
        out_shape=jax.ShapeDtypeStruct((b, 1, BRANCH_W), BF16),
        grid_spec=pltpu.PrefetchScalarGridSpec(
            num_scalar_prefetch=1, grid=(b, n_steps),
            in_specs=[tok] + pages + pages, out_specs=tok,
            scratch_shapes=[pltpu.VMEM(block, F32), pltpu.VMEM((H_SB, 1), F32),
                            pltpu.VMEM(block, F32)]),
        compiler_params=_params(("parallel", "arbitrary")),
        name="sb_decode",
    )(page_table, sq, *([cache_k] * PAGES_PER_STEP), *([cache_v] * PAGES_PER_STEP))


def _df_decode_kernel(pt_ref, q_ref, kn_ref, vn_ref, slope_ref, lv_ref, nw_ref, *refs,
                      n_steps, lam_init):
    k_refs = refs[:PAGES_PER_STEP]
    v_refs = refs[PAGES_PER_STEP:2 * PAGES_PER_STEP]
    o_ref, qc_ref, m_ref, l_ref, acc_ref = refs[2 * PAGES_PER_STEP:]
    g = pl.program_id(1)
    n_maps = 2 * H_DF
    page_rows = qc_ref.shape[-1]
    past = n_steps * PAGES_PER_STEP * page_rows
    q = q_ref[...].astype(F32) * (DH_DF ** -0.5)

    @pl.when(g == 0)
    def _():
        _store_lane_columns(q, qc_ref)
        m_ref[...] = jnp.full(m_ref.shape, NEG_BIG, F32)
        l_ref[...] = jnp.zeros(l_ref.shape, F32)
        acc_ref[...] = jnp.zeros(acc_ref.shape, F32)

    rows = PAGES_PER_STEP * n_maps
    slope = jnp.concatenate([slope_ref[...][:, 0:1]] * PAGES_PER_STEP, axis=0)
    page_of_row = lax.broadcasted_iota(jnp.int32, (rows, page_rows), 0) // n_maps
    within = lax.broadcasted_iota(jnp.int32, (rows, page_rows), 1)
    first_key = (g * PAGES_PER_STEP + page_of_row) * page_rows
    dist = (past - first_key - within).astype(F32)
    flat = H_DF * page_rows
    ek = lax.broadcasted_iota(jnp.int32, (page_rows, flat), 0)
    er = lax.broadcasted_iota(jnp.int32, (page_rows, flat), 1)
    expand = jnp.where(er // H_DF == ek, 1.0, 0.0).astype(BF16)
    mr = lax.broadcasted_iota(jnp.int32, (rows, flat), 0)
    mc = lax.broadcasted_iota(jnp.int32, (rows, flat), 1)
    own_head = mc % H_DF == (mr % n_maps) // 2
    m, l, acc = m_ref[...], l_ref[...], acc_ref[...]

    def over_pages(op, x):
        return functools.reduce(op, [x[i * n_maps:(i + 1) * n_maps] for i in range(PAGES_PER_STEP)])

    s = jnp.concatenate([_page_scores(k_ref, qc_ref) for k_ref in k_refs], axis=0) - slope * dist
    m_new = jnp.maximum(m, over_pages(jnp.maximum, jnp.max(s, axis=-1, keepdims=True)))
    p = jnp.exp(s - jnp.concatenate([m_new] * PAGES_PER_STEP, axis=0))
    alpha = jnp.exp(m - m_new)
    pe = jnp.where(own_head, _dot(p.astype(BF16), expand), 0.0)
    acc = alpha * acc
    for i, v_ref in enumerate(v_refs):
        acc = acc + _dot(pe[i * n_maps:(i + 1) * n_maps].astype(BF16),
                         v_ref[...].reshape(flat, DV_DF_W).astype(BF16))
    m_ref[...] = m_new
    l_ref[...] = alpha * l + over_pages(jnp.add, jnp.sum(p, axis=-1, keepdims=True))
    acc_ref[...] = acc

    @pl.when(g == n_steps - 1)
    def _():
        s_own = jnp.sum(_head_rows(q, n_maps, DH_DF) * kn_ref[...], axis=-1, keepdims=True)
        v_own = jnp.concatenate(
            [vn_ref[:, (r // 2) * DV_DF_W:(r // 2 + 1) * DV_DF_W] for r in range(n_maps)], axis=0)
        m_last, l_last, acc_last = m_ref[...], l_ref[...], acc_ref[...]
        mf = jnp.maximum(m_last, s_own)
        p_own = jnp.exp(s_own - mf)
        scale_last = jnp.exp(m_last - mf)
        lf = scale_last * l_last + p_own
        accf = scale_last * acc_last + p_own * v_own
        lam = _diff_lambda(lv_ref[...], lam_init)
        o_map = accf / lf
        for h in range(H_DF):
            o = o_map[2 * h:2 * h + 1, :] - lam * o_map[2 * h + 1:2 * h + 2, :]
            o_ref[:, h * DV_DF_W:(h + 1) * DV_DF_W] = (
                _rms(o, nw_ref[...]) * (1.0 - lam_init)).astype(BF16)


DV_DF_W = 2 * DH_DF


def _df_decode(dq, dk_new, dv_new, cache_k, cache_v, page_table, slopes8, lv, nw, layer,
               lam_init):
    b = dq.shape[0]
    n_pages = page_table.shape[1]
    kblock, vblock = cache_k.shape[2:], cache_v.shape[2:]
    assert n_pages % PAGES_PER_STEP == 0
    n_steps = n_pages // PAGES_PER_STEP
    tok = pl.BlockSpec((None, 1, BRANCH_W), lambda bi, g, pt: (bi, 0, 0))
    const = lambda shape: pl.BlockSpec(shape, lambda bi, g, pt: (0,) * len(shape))
    n_maps = 2 * H_DF
    return pl.pallas_call(
        functools.partial(_df_decode_kernel, n_steps=n_steps, lam_init=lam_init),
        out_shape=jax.ShapeDtypeStruct((b, 1, BRANCH_W), BF16),
        grid_spec=pltpu.PrefetchScalarGridSpec(
            num_scalar_prefetch=1, grid=(b, n_steps),
            in_specs=[tok, tok, tok, const(slopes8.shape), const(lv.shape), const(nw.shape)]
            + _page_specs(layer, n_pages, kblock, False)
            + _page_specs(layer, n_pages, vblock, False),
            out_specs=tok,
            scratch_shapes=[pltpu.VMEM(kblock, F32), pltpu.VMEM((n_maps, 1), F32),
                            pltpu.VMEM((n_maps, 1), F32), pltpu.VMEM((n_maps, DV_DF_W), F32)]),
        compiler_params=_params(("parallel", "arbitrary")),
        name="df_decode",
    )(page_table, dq, dk_new, dv_new, slopes8, lv, nw,
      *([cache_k] * PAGES_PER_STEP), *([cache_v] * PAGES_PER_STEP))


def _cache_views(cache_sb_k, cache_sb_v, cache_df_k, cache_df_v):
    d, n, p = cache_sb_k.shape[:3]
    keys_last = lambda c: jnp.transpose(
        c.reshape(d, n, p, BRANCH_W // DH_SB, DH_SB), (0, 1, 3, 4, 2))
    return (keys_last(cache_sb_k), keys_last(cache_sb_v), keys_last(cache_df_k),
            cache_df_v.reshape(d, n, p * H_DF // 8, 8, DV_DF_W))


def _pack_w_in(w):
    a = 9 * 512
    gab = w[:, a:a + 2 * H_GD]
    gz = w[:, a + 2 * H_GD:a + 2 * H_GD + 512]
    gates = w[:, a + 2 * H_GD + 512:]
    pad = jnp.zeros((w.shape[0], GAB_W - 2 * H_GD), w.dtype)
    return jnp.concatenate([w[:, :a], gz, gates, gab, pad], axis=1).astype(BF16)


def _lane_row(v):
    return jnp.pad(v.astype(F32), (0, LANES - v.shape[0]))[None, :]


def _row_tile(m):
    return min(m, 256)


def kernel(x_prompt, x_sample, cache_sb_k, cache_sb_v, cache_df_k, cache_df_v, state_gdn,
           state_gdn_conv, page_table, norm_w, ffn_w_in, ffn_w_out, w_in, diff_lambda,
           diff_norm_w, gdn_conv_w, gdn_a_log, gdn_dt_bias, gdn_norm_w, w_branch, w_out,
           final_norm_w):
    depth = w_in.shape[0]
    bp, tp, _ = x_prompt.shape
    bs, ts, _ = x_sample.shape
    assert ts == 1

    ffn_in = ffn_w_in.astype(BF16)
    ffn_out = ffn_w_out.astype(BF16)
    wb = w_branch.astype(BF16)
    wo = w_out.astype(BF16)
    fw = final_norm_w[None, :]
    slopes = 2.0 ** (-8.0 * jnp.arange(1, H_DF + 1, dtype=F32) / H_DF)
    slopes8 = jnp.broadcast_to(jnp.repeat(slopes, 2)[:, None], (2 * H_DF, LANES))
    ck, cv, dkc, dvc = _cache_views(cache_sb_k, cache_sb_v, cache_df_k, cache_df_v)

    xp = x_prompt.reshape(bp * tp, D_MODEL)
    xs = x_sample.reshape(bs, D_MODEL)
    tmp, tms = _row_tile(bp * tp), _row_tile(bs)
    p_states, s_states = [], []
    for l in range(depth):
        lam_init = 0.8 - 0.6 * math.exp(-0.3 * l)
        wl = _pack_w_in(w_in[l])
        nw = norm_w[l][:, None, :]
        prm = jnp.concatenate([_lane_row(gdn_a_log[l]), _lane_row(gdn_dt_bias[l]),
                               jnp.zeros((6, LANES), F32)], axis=0)
        gnw = gdn_norm_w[l][None, :]
        dnw = diff_norm_w[l][None, :]
        last = l == depth - 1

        xp = _ffn(xp, nw[0], ffn_in[l, 0], ffn_out[l, 0], fw, final=False, tm=tmp)
        sq, sk, sv, dq, dk, dv, xc, gz, gates, gab = _proj(xp, nw[1], wl, tm=tmp)
        r3 = lambda a: a.reshape(bp, tp, a.shape[-1])
        o_sb = _sb_prompt(r3(sq), r3(sk), r3(sv), tq=ATT_TQ)
        o_df = _df_prompt(r3(dq), r3(dk), r3(dv), slopes, diff_lambda[l], dnw, tq=ATT_TQ,
                          lam_init=lam_init)
        xc3 = r3(xc)
        o_gd, s_fin = _gdn_prompt(
            xc3, r3(gab), r3(gz), gdn_conv_w[l], prm, gnw,
            jnp.zeros((bp, 8, GD_CONV_CH), F32),
            jnp.zeros((bp, H_GD, DK_GD, DV_GD), F32))
        xp = _merge(xp, o_sb.reshape(bp * tp, BRANCH_W), o_df.reshape(bp * tp, BRANCH_W),
                    o_gd.reshape(bp * tp, BRANCH_W), gates, wb[l], wo[l], tm=tmp)
        xp = _ffn(xp, nw[2], ffn_in[l, 1], ffn_out[l, 1], fw, final=last, tm=tmp)
        p_states.append((
            sk.reshape(bp, tp, H_SB, DH_SB), sv.reshape(bp, tp, H_SB, DH_SB),
            dk.reshape(bp, tp, H_DF, 2, DH_DF), dv.reshape(bp, tp, H_DF, DV_DF_W),
            s_fin, xc3[:, tp - (CONV_W - 1):, :]))

        xs = _ffn(xs, nw[0], ffn_in[l, 0], ffn_out[l, 0], fw, final=False, tm=tms)
        sq, sk, sv, dq, dk, dv, xc, gz, gates, gab = _proj(xs, nw[1], wl, tm=tms)
        r3 = lambda a: a.reshape(bs, 1, a.shape[-1])
        o_sb = _sb_decode(r3(sq), ck, cv, page_table, l)
        o_df = _df_decode(r3(dq), r3(dk), r3(dv), dkc, dvc, page_table, slopes8,
                          diff_lambda[l], dnw, l, lam_init)
        o_gd, s_new = _gdn_step(r3(xc), r3(gab), r3(gz), gdn_conv_w[l], prm, gnw,
                                state_gdn_conv[l], state_gdn[l])
        xs = _merge(xs, o_sb.reshape(bs, BRANCH_W), o_df.reshape(bs, BRANCH_W),
                    o_gd.reshape(bs, BRANCH_W), gates, wb[l], wo[l], tm=tms)
        xs = _ffn(xs, nw[2], ffn_in[l, 1], ffn_out[l, 1], fw, final=last, tm=tms)
        new_buf = jnp.concatenate([state_gdn_conv[l][:, 1:, :], r3(xc)], axis=1)
        s_states.append((
            sk.reshape(bs, 1, H_SB, DH_SB), sv.reshape(bs, 1, H_SB, DH_SB),
            dk.reshape(bs, 1, H_DF, 2, DH_DF), dv.reshape(bs, 1, H_DF, DV_DF_W),
            s_new, new_buf))

    p_out = [jnp.stack(a) for a in zip(*p_states)]
    s_out = [jnp.stack(a) for a in zip(*s_states)]
    return (xp.reshape(bp, tp, D_MODEL), xs.reshape(bs, 1, D_MODEL), *p_out, *s_out)
```

```python
import functools
import math

import jax
import jax.numpy as jnp
from jax import lax
from jax.experimental import pallas as pl
from jax.experimental.pallas import tpu as pltpu

F32 = jnp.float32
BF16 = jnp.bfloat16

D_MODEL = 1024
DH_SB = 64
H_SB = 8
DH_DF = 64
H_DF = 4
DK_GD = 128
DV_GD = 128
H_GD = 4
CONV_W = 4
GD_CONV_CH = 2 * H_GD * DK_GD + H_GD * DV_GD
GDN_CHUNK = 64
GDN_CHUNKS_PER_STEP = 4
BRANCH_W = 512
N_BRANCH = 3
D_FF = 2816
NORM_EPS = 1e-6
L2_EPS = 1e-6
NEG_BIG = -1e30

LANES = 128
FF_CHUNK = 256
GAB_W = LANES
VMEM_LIMIT = 56 * 1024 * 1024

PROJ_OUTS = (
    ("sq", 512, BF16), ("sk", 512, F32), ("sv", 512, F32),
    ("dq", 512, BF16), ("dk", 512, F32), ("dv", 512, F32),
    ("xc", GD_CONV_CH, F32), ("gz", 512, F32),
    ("gates", N_BRANCH * D_MODEL, F32), ("gab", GAB_W, F32),
)
PROJ_COLS = sum(w for _, w, _ in PROJ_OUTS)


def _dot(a, b, precision=None):
    return jnp.dot(a, b, preferred_element_type=F32, precision=precision)


def _dot_nt(a, b, precision=None):
    return lax.dot_general(a, b, (((1,), (1,)), ((), ())),
                           preferred_element_type=F32, precision=precision)


def _dot_tn(a, b, precision=None):
    return lax.dot_general(a, b, (((0,), (0,)), ((), ())),
                           preferred_element_type=F32, precision=precision)


_HI = lax.Precision.HIGHEST


def _rms(x, w):
    ms = jnp.mean(x * x, axis=-1, keepdims=True)
    return x * lax.rsqrt(ms + NORM_EPS) * w


def _sigmoid(x):
    return 1.0 / (1.0 + jnp.exp(-x))


def _silu(x):
    return x * _sigmoid(x)


def _softplus(x):
    return jnp.maximum(x, 0.0) + jnp.log(1.0 + jnp.exp(-jnp.abs(x)))


def _params(sem):
    return pltpu.CompilerParams(dimension_semantics=sem, vmem_limit_bytes=VMEM_LIMIT)


def _resident(shape):
    nd = len(shape)
    return pl.BlockSpec(shape, lambda *_: (0,) * nd, pipeline_mode=pl.Buffered(1))


def _ffn_kernel(x_ref, nw_ref, win_ref, wout_ref, fw_ref, o_ref, *, final):
    x = x_ref[...]
    h = _rms(x, nw_ref[...]).astype(BF16)
    acc = jnp.zeros(x.shape, F32)
    for c in range(D_FF // FF_CHUNK):
        lo = c * FF_CHUNK
        g = _dot(h, win_ref[:, lo:lo + FF_CHUNK])
        u = _dot(h, win_ref[:, D_FF + lo:D_FF + lo + FF_CHUNK])
        a = (_silu(g) * u).astype(BF16)
        acc = acc + _dot(a, wout_ref[lo:lo + FF_CHUNK, :])
    y = x + 0.5 * acc
    if final:
        y = _rms(y, fw_ref[...])
    o_ref[...] = y


def _ffn(x, nw, w_in, w_out, fw, *, final, tm):
    m = x.shape[0]
    row = pl.BlockSpec((tm, D_MODEL), lambda i: (i, 0))
    return pl.pallas_call(
        functools.partial(_ffn_kernel, final=final),
        out_shape=jax.ShapeDtypeStruct((m, D_MODEL), F32),
        grid=(m // tm,),
        in_specs=[row, _resident((1, D_MODEL)), _resident(w_in.shape),
                  _resident(w_out.shape), _resident((1, D_MODEL))],
        out_specs=row,
        compiler_params=_params(("parallel",)),
        name="ffn",
    )(x, nw, w_in, w_out, fw)


def _proj_kernel(x_ref, nw_ref, w_ref, *o_refs):
    h = _rms(x_ref[...], nw_ref[...]).astype(BF16)
    off = 0
    for (_, width, dt), o_ref in zip(PROJ_OUTS, o_refs):
        for lo in range(0, width, 512):
            n = min(512, width - lo)
            o_ref[:, lo:lo + n] = _dot(h, w_ref[:, off + lo:off + lo + n]).astype(dt)
        off += width


def _proj(x, nw, w, *, tm):
    m = x.shape[0]
    return pl.pallas_call(
        _proj_kernel,
        out_shape=[jax.ShapeDtypeStruct((m, wd), dt) for _, wd, dt in PROJ_OUTS],
        grid=(m // tm,),
        in_specs=[pl.BlockSpec((tm, D_MODEL), lambda i: (i, 0)),
                  _resident((1, D_MODEL)), _resident(w.shape)],
        out_specs=[pl.BlockSpec((tm, wd), lambda i: (i, 0)) for _, wd, _ in PROJ_OUTS],
        compiler_params=_params(("parallel",)),
        name="proj",
    )(x, nw, w)


def _merge_kernel(x_ref, osb_ref, odf_ref, ogd_ref, gates_ref, wb_ref, wo_ref, o_ref):
    m = None
    for i, b_ref in enumerate((osb_ref, odf_ref, ogd_ref)):
        p = _dot(b_ref[...], wb_ref[i])
        t = _sigmoid(gates_ref[:, i * D_MODEL:(i + 1) * D_MODEL]) * p
        m = t if m is None else m + t
    o_ref[...] = x_ref[...] + _dot(m.astype(BF16), wo_ref[...])


def _merge(x, o_sb, o_df, o_gd, gates, wb, wo, *, tm):
    m = x.shape[0]
    row = lambda w: pl.BlockSpec((tm, w), lambda i: (i, 0))
    return pl.pallas_call(
        _merge_kernel,
        out_shape=jax.ShapeDtypeStruct((m, D_MODEL), F32),
        grid=(m // tm,),
        in_specs=[row(D_MODEL), row(BRANCH_W), row(BRANCH_W), row(BRANCH_W),
                  row(N_BRANCH * D_MODEL), _resident(wb.shape), _resident(wo.shape)],
        out_specs=row(D_MODEL),
        compiler_params=_params(("parallel",)),
        name="merge",
    )(x, o_sb, o_df, o_gd, gates, wb, wo)


def _log_sigmoid_pair(z):
    ls = jnp.minimum(z, 0.0) - jnp.log(1.0 + jnp.exp(-jnp.abs(z)))
    return ls, ls - z


def _suffix_sums(x, upper):
    hi = x.astype(BF16)
    lo = (x - hi.astype(F32)).astype(BF16)
    return _dot(hi, upper) + _dot(lo, upper)


ATT_TK = 256
ATT_TQ = 512
DF_BLOCKS_PER_TRIP = 4
SB_DONE = -110.0


def _stack_halves(q, seg):
    lane = lax.broadcasted_iota(jnp.int32, q.shape, 1)
    return jnp.concatenate(
        [jnp.where(lane // seg == j, q, 0.0) for j in range(LANES // seg)], axis=0).astype(BF16)


def _walk_key_blocks(qi, tq, blocks, state, per_trip):
    nd = tq // ATT_TK
    state = blocks([(qi + 1) * nd - 1 - d for d in range(nd)], state, True)
    n_free = qi * nd

    def full_trip(i, st):
        kj = n_free - 1 - per_trip * i
        return blocks([kj - d for d in range(per_trip)], st, False)

    def last_one(i, st):
        return blocks([n_free % per_trip - 1 - i], st, False)

    state = lax.fori_loop(0, n_free // per_trip, full_trip, state)
    return lax.fori_loop(0, n_free % per_trip, last_one, state)


def _sb_prompt_kernel(q_ref, k_ref, v_ref, o_ref, kb_ref, vb_ref, *, tq):
    qi = pl.program_id(2)
    tk = ATT_TK

    @pl.when(qi == 0)
    def _():
        kb_ref[...] = k_ref[...].astype(BF16)
        vb_ref[...] = v_ref[...].astype(BF16)

    qs = _stack_halves(q_ref[...].astype(F32) * (DH_SB ** -0.5), DH_SB)
    rows = qs.shape[0]
    rowq = lax.broadcasted_iota(jnp.int32, (rows, tk), 0) % tq
    col = lax.broadcasted_iota(jnp.int32, (rows, tk), 1)
    ur = lax.broadcasted_iota(jnp.int32, (tk, tk), 0)
    uc = lax.broadcasted_iota(jnp.int32, (tk, tk), 1)
    upper = jnp.where(ur > uc, 1.0, 0.0).astype(BF16)

    def blocks(kjs, state, masked):
        carry, acc = state
        starts = [pl.multiple_of(kj * tk, tk) for kj in kjs]
        pairs = [_log_sigmoid_pair(_dot_nt(qs, kb_ref[pl.ds(s, tk), :])) for s in starts]
        lss = [p[0] for p in pairs]
        l1ms = [p[1] for p in pairs]
        if masked:
            befores = [col + (kj * tk - qi * tq) < rowq for kj in kjs]
            l1ms = [jnp.where(b, x, 0.0) for b, x in zip(befores, l1ms)]
        sufs = [_suffix_sums(x, upper) for x in l1ms]
        carries = [carry]
        for x in l1ms:
            carries.append(carries[-1] + jnp.sum(x, axis=-1, keepdims=True))
        probs = [jnp.exp(ls + suf + c) for ls, suf, c in zip(lss, sufs, carries)]
        if masked:
            probs = [jnp.where(b, a, 0.0) for b, a in zip(befores, probs)]
        for a, s in zip(probs, starts):
            acc = acc + _dot(a.astype(BF16), vb_ref[pl.ds(s, tk), :])
        return carries[-1], acc

    nd = tq // tk
    state = blocks([(qi + 1) * nd - 1 - d for d in range(nd)],
                   (jnp.zeros((rows, 1), F32), jnp.zeros((rows, LANES), F32)), True)

    def unfinished(loop):
        kj, carry, _ = loop
        return jnp.logical_and(kj >= 0, jnp.max(carry) > SB_DONE)

    def one_block(loop):
        kj, carry, acc = loop
        carry, acc = blocks([kj], (carry, acc), False)
        return kj - 1, carry, acc

    _, _, acc = lax.while_loop(unfinished, one_block, (qi * nd - 1,) + state)
    lane = lax.broadcasted_iota(jnp.int32, (tq, LANES), 1)
    o_ref[...] = jnp.where(lane < DH_SB, acc[:tq], acc[tq:]).astype(BF16)


def _sb_prompt(sq, sk, sv, *, tq):
    b, t, _ = sq.shape
    assert tq % ATT_TK == 0 and t % tq == 0
    qspec = pl.BlockSpec((None, tq, LANES), lambda bi, hp, qi: (bi, qi, hp))
    kspec = pl.BlockSpec((None, t, LANES), lambda bi, hp, qi: (bi, 0, hp))
    return pl.pallas_call(
        functools.partial(_sb_prompt_kernel, tq=tq),
        out_shape=jax.ShapeDtypeStruct((b, t, BRANCH_W), BF16),
        grid=(b, BRANCH_W // LANES, t // tq),
        in_specs=[qspec, kspec, kspec],
        out_specs=qspec,
        scratch_shapes=[pltpu.VMEM((t, LANES), BF16), pltpu.VMEM((t, LANES), BF16)],
        compiler_params=_params(("parallel", "parallel", "arbitrary")),
        name="sb_prompt",
    )(sq, sk, sv)


def _diff_lambda(lv, lam_init):
    a = jnp.sum(lv[0:1] * lv[1:2], axis=-1, keepdims=True)
    b = jnp.sum(lv[2:3] * lv[3:4], axis=-1, keepdims=True)
    return jnp.exp(a) - jnp.exp(b) + lam_init


def _df_prompt_kernel(slope_ref, q_ref, k_ref, v_ref, lv_ref, nw_ref, o_ref, kb_ref, vb_ref,
                      *, tq, lam_init):
    h = pl.program_id(1)
    qi = pl.program_id(2)

    tk = ATT_TK
    slope = slope_ref[h]
    t_all = k_ref.shape[0]

    @pl.when(qi == 0)
    def _():
        kb_ref[:, 0:LANES] = k_ref[...].astype(BF16)
        vb_ref[...] = v_ref[...].astype(BF16)
        pos = lax.broadcasted_iota(jnp.int32, (t_all, LANES), 0).astype(F32) * slope
        hi, lo = _two_terms(pos)
        lane = lax.broadcasted_iota(jnp.int32, (t_all, LANES), 1)
        kb_ref[:, LANES:2 * LANES] = jnp.where(
            lane == 0, hi, jnp.where(lane == 1, lo, jnp.where(lane == 2, 1.0, 0.0).astype(BF16)))

    qs = _stack_halves(q_ref[...].astype(F32) * (DH_DF ** -0.5), DH_DF)
    rows = qs.shape[0]
    qlane = lax.broadcasted_iota(jnp.int32, (rows, LANES), 1)
    q_bias = jnp.where(qlane < 2, 1.0, jnp.where(qlane == 2, -slope * (qi * tq).astype(F32), 0.0))
    qs = jnp.concatenate([qs, q_bias.astype(BF16)], axis=1)
    rowq = lax.broadcasted_iota(jnp.int32, (rows, tk), 0) % tq
    col = lax.broadcasted_iota(jnp.int32, (rows, tk), 1)

    def block(kj, state, masked):
        m, l, acc = state
        start = pl.multiple_of(kj * tk, tk)
        offset = qi * tq - kj * tk
        s = _dot_nt(qs, kb_ref[pl.ds(start, tk), :])
        if masked:
            visible = col - offset <= rowq
            s = jnp.where(visible, s, NEG_BIG)
        m_new = jnp.maximum(m, jnp.max(s, axis=-1, keepdims=True))
        p = jnp.exp(s - m_new)
        if masked:
            p = jnp.where(visible, p, 0.0)
        alpha = jnp.exp(m - m_new)
        l = alpha * l + jnp.sum(p, axis=-1, keepdims=True)
        acc = alpha * acc + _dot(p.astype(BF16), vb_ref[pl.ds(start, tk), :])
        return m_new, l, acc

    def blocks(kjs, state, masked):
        for kj in kjs:
            state = block(kj, state, masked)
        return state

    _, l, acc = _walk_key_blocks(
        qi, tq, blocks, (jnp.full((rows, 1), NEG_BIG, F32), jnp.zeros((rows, 1), F32),
                         jnp.zeros((rows, LANES), F32)), DF_BLOCKS_PER_TRIP)
    lam = _diff_lambda(lv_ref[...], lam_init)
    o = acc[:tq] / l[:tq] - lam * (acc[tq:] / l[tq:])
    o_ref[...] = (_rms(o, nw_ref[...]) * (1.0 - lam_init)).astype(BF16)


def _df_prompt(dq, dk, dv, slopes, lv, nw, *, tq, lam_init):
    b, t, _ = dq.shape
    assert tq % ATT_TK == 0 and t % tq == 0
    qspec = pl.BlockSpec((None, tq, LANES), lambda bi, h, qi: (bi, qi, h))
    kspec = pl.BlockSpec((None, t, LANES), lambda bi, h, qi: (bi, 0, h))
    return pl.pallas_call(
        functools.partial(_df_prompt_kernel, tq=tq, lam_init=lam_init),
        out_shape=jax.ShapeDtypeStruct((b, t, BRANCH_W), BF16),
        grid=(b, H_DF, t // tq),
        in_specs=[pl.BlockSpec(memory_space=pltpu.SMEM), qspec, kspec, kspec,
                  _resident(lv.shape), _resident(nw.shape)],
        out_specs=qspec,
        scratch_shapes=[pltpu.VMEM((t, 2 * LANES), BF16), pltpu.VMEM((t, LANES), BF16)],
        compiler_params=_params(("parallel", "parallel", "arbitrary")),
        name="df_prompt",
    )(slopes, dq, dk, dv, lv, nw)


def _gdn_gates(gab, prm):
    g = -jnp.exp(prm[0:1, :]) * _softplus(gab + prm[1:2, :])
    return g, _sigmoid(gab)


def _l2n(x):
    return x * lax.rsqrt(jnp.sum(x * x, axis=-1, keepdims=True) + L2_EPS)


def _two_terms(x):
    hi = x.astype(BF16)
    return hi, (x - hi.astype(F32)).astype(BF16)


def _split_lhs(a):
    hi, lo = _two_terms(a)
    return jnp.concatenate([hi, hi, lo], axis=1)


def _split_rhs(b):
    hi, lo = _two_terms(b)
    return jnp.concatenate([hi, lo, hi], axis=0)


def _gdn_out(o, nw, z):
    return (_rms(o, nw) * _silu(z)).astype(BF16)


def _gdn_prompt_kernel(xc_ref, gab_ref, gz_ref, cw_ref, prm_ref, nw_ref, cb_ref, s0_ref,
                       o_ref, sfin_ref, buf_ref, st_ref, *, n_steps):
    c = pl.program_id(1)
    ch = GDN_CHUNK
    rows = GDN_CHUNKS_PER_STEP * ch

    @pl.when(c == 0)
    def _():
        buf_ref[0:8, :] = cb_ref[...]
        st_ref[...] = s0_ref[...]

    buf_ref[8:8 + rows, :] = xc_ref[...]
    xconv = None
    for j in range(CONV_W):
        t = cw_ref[j:j + 1, :] * buf_ref[5 + j:5 + j + rows, :]
        xconv = t if xconv is None else xconv + t
    buf_ref[0:8, :] = buf_ref[rows:rows + 8, :]
    xs = _silu(xconv)

    g_all, beta_all = _gdn_gates(gab_ref[...], prm_ref[...])
    row = lax.broadcasted_iota(jnp.int32, (ch, ch), 0)
    col = lax.broadcasted_iota(jnp.int32, (ch, ch), 1)
    tri = col <= row
    strict = col < row
    eye = jnp.where(row == col, 1.0, 0.0)
    lower = jnp.where(tri, 1.0, 0.0)
    hk = H_GD * DK_GD

    problems = [(j, h) for j in range(GDN_CHUNKS_PER_STEP) for h in range(H_GD)]
    cums = []
    for j in range(GDN_CHUNKS_PER_STEP):
        cum_all = _dot(lower, g_all[j * ch:(j + 1) * ch], _HI)
        cums.append((cum_all, cum_all.T))

    def first_stage(j, h):
        r0 = j * ch
        q = _l2n(xs[r0:r0 + ch, h * DK_GD:(h + 1) * DK_GD]) * (DK_GD ** -0.5)
        k = _l2n(xs[r0:r0 + ch, hk + h * DK_GD:hk + (h + 1) * DK_GD])
        v = xs[r0:r0 + ch, 2 * hk + h * DV_GD:2 * hk + (h + 1) * DV_GD]
        gc = cums[j][0][:, h:h + 1]
        gr = cums[j][1][h:h + 1, :]
        beta = beta_all[r0:r0 + ch, H_GD + h:H_GD + h + 1]
        decay = jnp.exp(jnp.where(tri, gc - gr, NEG_BIG))
        eg = jnp.exp(gc)
        g_last = gc[ch - 1:ch, :]
        kb = k.astype(BF16)
        qk = (_dot_nt(q.astype(BF16), kb) * decay).astype(BF16)
        pw = -jnp.where(strict, beta * decay * _dot_nt(kb, kb), 0.0)
        rhs = _split_rhs(jnp.concatenate([beta * v, (beta * eg) * k], axis=1))
        carry_over = ((q * eg).astype(BF16), qk, (k * jnp.exp(g_last - gc)).astype(BF16),
                      jnp.exp(g_last))
        return pw, rhs, carry_over

    staged = [first_stage(j, h) for j, h in problems]
    pws = [p[0] for p in staged]
    invs = [eye + pw for pw in pws]
    for _ in range(5):
        pws = [_dot(_split_lhs(pw), _split_rhs(pw)) for pw in pws]
        invs = [inv + _dot(_split_lhs(inv), _split_rhs(pw)) for inv, pw in zip(invs, pws)]
    sols = [_dot(_split_lhs(inv), p[1]) for inv, p in zip(invs, staged)]
    prepared = {jh: (sol[:, :DV_GD], sol[:, DV_GD:].astype(BF16)) + p[2]
                for jh, sol, p in zip(problems, sols, staged)}

    states = [st_ref[h] for h in range(H_GD)]
    for j in range(GDN_CHUNKS_PER_STEP):
        for h in range(H_GD):
            sol_v, sol_k, q_eg, qk, k_dec, e_last = prepared[(j, h)]
            s = states[h]
            sb = s.astype(BF16)
            ub = (sol_v - _dot(sol_k, sb)).astype(BF16)
            o = _dot(q_eg, sb) + _dot(qk, ub)
            states[h] = e_last * s + _dot_tn(k_dec, ub)
            o_ref[j * ch:(j + 1) * ch, h * DV_GD:(h + 1) * DV_GD] = _gdn_out(
                o, nw_ref[...], gz_ref[j * ch:(j + 1) * ch, h * DV_GD:(h + 1) * DV_GD])
    for h in range(H_GD):
        st_ref[h] = states[h]

    @pl.when(c == n_steps - 1)
    def _():
        sfin_ref[...] = st_ref[...]


def _gdn_prompt(xc, gab, gz, cw, prm, nw, cb8, s0):
    b, t, _ = xc.shape
    rows = GDN_CHUNKS_PER_STEP * GDN_CHUNK
    assert t % rows == 0
    n_steps = t // rows
    tok = lambda w: pl.BlockSpec((None, rows, w), lambda bi, c: (bi, c, 0))
    st = pl.BlockSpec((None, H_GD, DK_GD, DV_GD), lambda bi, c: (bi, 0, 0, 0))
    return pl.pallas_call(
        functools.partial(_gdn_prompt_kernel, n_steps=n_steps),
        out_shape=[jax.ShapeDtypeStruct((b, t, BRANCH_W), BF16),
                   jax.ShapeDtypeStruct((b, H_GD, DK_GD, DV_GD), F32)],
        grid=(b, n_steps),
        in_specs=[tok(GD_CONV_CH), tok(GAB_W), tok(BRANCH_W), _resident(cw.shape),
                  _resident(prm.shape), _resident(nw.shape),
                  pl.BlockSpec((None, 8, GD_CONV_CH), lambda bi, c: (bi, 0, 0)), st],
        out_specs=[tok(BRANCH_W), st],
        scratch_shapes=[pltpu.VMEM((rows + 8, GD_CONV_CH), F32),
                        pltpu.VMEM((H_GD, DK_GD, DV_GD), F32)],
        compiler_params=_params(("parallel", "arbitrary")),
        name="gdn_prompt",
    )(xc, gab, gz, cw, prm, nw, cb8, s0)


def _gdn_step_kernel(xc_ref, gab_ref, gz_ref, cw_ref, prm_ref, nw_ref, cb_ref, s0_ref,
                     o_ref, sn_ref):
    xconv = cw_ref[CONV_W - 1:CONV_W, :] * xc_ref[...]
    for j in range(CONV_W - 1):
        xconv = xconv + cw_ref[j:j + 1, :] * cb_ref[j:j + 1, :]
    xs = _silu(xconv)
    g_all, beta_all = _gdn_gates(gab_ref[...], prm_ref[...])
    first = lax.broadcasted_iota(jnp.int32, (8, DK_GD), 0) == 0
    rows8 = lambda x: jnp.where(first, jnp.broadcast_to(x, (8, x.shape[-1])), 0.0)
    hk = H_GD * DK_GD
    for h in range(H_GD):
        q = rows8(_l2n(xs[:, h * DK_GD:(h + 1) * DK_GD]) * (DK_GD ** -0.5))
        k = rows8(_l2n(xs[:, hk + h * DK_GD:hk + (h + 1) * DK_GD]))
        v = xs[:, 2 * hk + h * DV_GD:2 * hk + (h + 1) * DV_GD]
        eg = jnp.exp(g_all[:, h:h + 1])
        beta = beta_all[:, H_GD + h:H_GD + h + 1]
        s = s0_ref[h]
        u = beta * (v - eg * _dot(k, s, _HI))
        s_new = eg * s + _dot_tn(k, u, _HI)
        sn_ref[h] = s_new
        o = _dot(q, s_new, _HI)[0:1, :]
        o_ref[:, h * DV_GD:(h + 1) * DV_GD] = _gdn_out(
            o, nw_ref[...], gz_ref[:, h * DV_GD:(h + 1) * DV_GD])


def _gdn_step(xc, gab, gz, cw, prm, nw, cb, s0):
    b = xc.shape[0]
    tok = lambda w: pl.BlockSpec((None, 1, w), lambda bi: (bi, 0, 0))
    st = pl.BlockSpec((None, H_GD, DK_GD, DV_GD), lambda bi: (bi, 0, 0, 0))
    return pl.pallas_call(
        _gdn_step_kernel,
        out_shape=[jax.ShapeDtypeStruct((b, 1, BRANCH_W), BF16),
                   jax.ShapeDtypeStruct((b, H_GD, DK_GD, DV_GD), F32)],
        grid=(b,),
        in_specs=[tok(GD_CONV_CH), tok(GAB_W), tok(BRANCH_W), _resident(cw.shape),
                  _resident(prm.shape), _resident(nw.shape),
                  pl.BlockSpec((None, CONV_W - 1, GD_CONV_CH), lambda bi: (bi, 0, 0)), st],
        out_specs=[tok(BRANCH_W), st],
        compiler_params=_params(("parallel",)),
        name="gdn_step",
    )(xc, gab, gz, cw, prm, nw, cb, s0)


PAGES_PER_STEP = 8


def _page_specs(layer, n_pages, block, descending):
    specs = []
    for i in range(PAGES_PER_STEP):
        def index(b, g, pt, i=i):
            p = g * PAGES_PER_STEP + i
            if descending:
                p = n_pages - 1 - p
            return (layer, pt[b, p]) + (0,) * len(block)
        specs.append(pl.BlockSpec((None, None) + block, index))
    return specs


def _store_lane_columns(q, qc_ref):
    n_seg, seg, lanes = qc_ref.shape
    first = lax.broadcasted_iota(jnp.int32, (8, seg), 0) == 0
    ones = jnp.where(lax.broadcasted_iota(jnp.int32, (8, lanes), 0) == 0, 1.0, 0.0).astype(BF16)
    for s in range(n_seg):
        qs = jnp.where(first, jnp.broadcast_to(q[:, s * seg:(s + 1) * seg], (8, seg)), 0.0)
        qc_ref[s] = _dot_tn(qs.astype(BF16), ones)


def _page_scores(k_ref, qc_ref):
    return jnp.concatenate(
        [jnp.sum(k_ref[s] * qc_ref[s], axis=0, keepdims=True) for s in range(k_ref.shape[0])],
        axis=0)


def _head_rows(q, n_rows, seg):
    w = q.shape[-1]
    r = lax.broadcasted_iota(jnp.int32, (n_rows, w), 0)
    c = lax.broadcasted_iota(jnp.int32, (n_rows, w), 1)
    return jnp.where(c // seg == r, jnp.broadcast_to(q, (n_rows, w)), 0.0)


SB_PAGES_PER_GROUP = 2


def _sb_decode_kernel(pt_ref, q_ref, k_hbm, v_hbm, o_ref, kbuf, vbuf, sems, qc_ref, acc_ref,
                      *, layer, n_pages):
    b = pl.program_id(0)
    grp = SB_PAGES_PER_GROUP
    n_groups = n_pages // grp
    page_rows = qc_ref.shape[-1]

    def copies(g):
        slot = g % 2
        out = []
        for i in range(grp):
            page = pt_ref[b, n_pages - 1 - (g * grp + i)]
            out.append(pltpu.make_async_copy(k_hbm.at[layer, page], kbuf.at[slot, i], sems.at[slot, 0, i]))
            out.append(pltpu.make_async_copy(v_hbm.at[layer, page], vbuf.at[slot, i], sems.at[slot, 1, i]))
        return out

    for c in copies(0):
        c.start()
    _store_lane_columns(q_ref[...].astype(F32) * (DH_SB ** -0.5), qc_ref)
    acc_ref[...] = jnp.zeros(acc_ref.shape, F32)
    row = lax.broadcasted_iota(jnp.int32, (page_rows, page_rows), 0)
    col = lax.broadcasted_iota(jnp.int32, (page_rows, page_rows), 1)
    upper = jnp.where(row > col, 1.0, 0.0).astype(BF16)

    def unfinished(loop):
        g, carry = loop
        return jnp.logical_and(g < n_groups, jnp.max(carry) > SB_DONE)

    def one_group(loop):
        g, carry = loop
        slot = g % 2

        @pl.when(g + 1 < n_groups)
        def _():
            for c in copies(g + 1):
                c.start()

        for c in copies(g):
            c.wait()
        z = jnp.concatenate([_page_scores(kbuf.at[slot, i], qc_ref) for i in range(grp)], axis=0)
        ls, l1m = _log_sigmoid_pair(z)
        page_sums = jnp.sum(l1m, axis=-1, keepdims=True)
        carries = [carry]
        for i in range(grp):
            carries.append(carries[-1] + page_sums[i * H_SB:(i + 1) * H_SB])
        a = jnp.exp(ls + _suffix_sums(l1m, upper) + jnp.concatenate(carries[:-1], axis=0))
        for h in range(H_SB):
            t = acc_ref[h]
            for i in range(grp):
                t = t + vbuf[slot, i, h] * a[i * H_SB + h:i * H_SB + h + 1, :]
            acc_ref[h] = t
        return g + 1, carries[-1]

    g_end, _ = lax.while_loop(unfinished, one_group, (0, jnp.zeros((H_SB, 1), F32)))

    @pl.when(g_end < n_groups)
    def _():
        for c in copies(g_end):
            c.wait()

    ones = jnp.ones((8, page_rows), F32)
    o_ref[...] = jnp.concatenate(
        [_dot_nt(ones, acc_ref[h], _HI)[0:1, :] for h in range(H_SB)], axis=1).astype(BF16)


def _sb_decode(sq, cache_k, cache_v, page_table, layer):
    b = sq.shape[0]
    n_pages = page_table.shape[1]
    block = cache_k.shape[2:]
    grp = SB_PAGES_PER_GROUP
    assert n_pages % grp == 0
    tok = pl.BlockSpec((None, 1, BRANCH_W), lambda bi, pt: (bi, 0, 0))
    hbm = pl.BlockSpec(memory_space=pl.ANY)
    return pl.pallas_call(
        functools.partial(_sb_decode_kernel, layer=layer, n_pages=n_pages),
        out_shape=jax.ShapeDtypeStruct((b, 1, BRANCH_W), BF16),
        grid_spec=pltpu.PrefetchScalarGridSpec(
            num_scalar_prefetch=1, grid=(b,),
            in_specs=[tok, hbm, hbm], out_specs=tok,
            scratch_shapes=[pltpu.VMEM((2, grp) + block, F32), pltpu.VMEM((2, grp) + block, F32),
                            pltpu.SemaphoreType.DMA((2, 2, grp)),
                            pltpu.VMEM(block, F32), pltpu.VMEM(block, F32)]),
        compiler_params=_params(("arbitrary",)),
        name="sb_decode",
    )(page_table, sq, cache_k, cache_v)


def _df_decode_kernel(pt_ref, q_ref, kn_ref, vn_ref, slope_ref, lv_ref, nw_ref, *refs,
                      n_steps, lam_init):
    k_refs = refs[:PAGES_PER_STEP]
    v_refs = refs[PAGES_PER_STEP:2 * PAGES_PER_STEP]
    o_ref, qc_ref, m_ref, l_ref, acc_ref = refs[2 * PAGES_PER_STEP:]
    g = pl.program_id(1)
    n_maps = 2 * H_DF
    page_rows = qc_ref.shape[-1]
    past = n_steps * PAGES_PER_STEP * page_rows
    q = q_ref[...].astype(F32) * (DH_DF ** -0.5)

    @pl.when(g == 0)
    def _():
        _store_lane_columns(q, qc_ref)
        m_ref[...] = jnp.full(m_ref.shape, NEG_BIG, F32)
        l_ref[...] = jnp.zeros(l_ref.shape, F32)
        acc_ref[...] = jnp.zeros(acc_ref.shape, F32)

    rows = PAGES_PER_STEP * n_maps
    slope = jnp.concatenate([slope_ref[...][:, 0:1]] * PAGES_PER_STEP, axis=0)
    page_of_row = lax.broadcasted_iota(jnp.int32, (rows, page_rows), 0) // n_maps
    within = lax.broadcasted_iota(jnp.int32, (rows, page_rows), 1)
    first_key = (g * PAGES_PER_STEP + page_of_row) * page_rows
    dist = (past - first_key - within).astype(F32)
    flat = H_DF * page_rows
    ek = lax.broadcasted_iota(jnp.int32, (page_rows, flat), 0)
    er = lax.broadcasted_iota(jnp.int32, (page_rows, flat), 1)
    expand = jnp.where(er // H_DF == ek, 1.0, 0.0).astype(BF16)
    mr = lax.broadcasted_iota(jnp.int32, (rows, flat), 0)
    mc = lax.broadcasted_iota(jnp.int32, (rows, flat), 1)
    own_head = mc % H_DF == (mr % n_maps) // 2
    m, l, acc = m_ref[...], l_ref[...], acc_ref[...]

    def over_pages(op, x):
        return functools.reduce(op, [x[i * n_maps:(i + 1) * n_maps] for i in range(PAGES_PER_STEP)])

    s = jnp.concatenate([_page_scores(k_ref, qc_ref) for k_ref in k_refs], axis=0) - slope * dist
    m_new = jnp.maximum(m, over_pages(jnp.maximum, jnp.max(s, axis=-1, keepdims=True)))
    p = jnp.exp(s - jnp.concatenate([m_new] * PAGES_PER_STEP, axis=0))
    alpha = jnp.exp(m - m_new)
    pe = jnp.where(own_head, _dot(p.astype(BF16), expand), 0.0)
    acc = alpha * acc
    for i, v_ref in enumerate(v_refs):
        acc = acc + _dot(pe[i * n_maps:(i + 1) * n_maps].astype(BF16),
                         v_ref[...].reshape(flat, DV_DF_W).astype(BF16))
    m_ref[...] = m_new
    l_ref[...] = alpha * l + over_pages(jnp.add, jnp.sum(p, axis=-1, keepdims=True))
    acc_ref[...] = acc

    @pl.when(g == n_steps - 1)
    def _():
        s_own = jnp.sum(_head_rows(q, n_maps, DH_DF) * kn_ref[...], axis=-1, keepdims=True)
        v_own = jnp.concatenate(
            [vn_ref[:, (r // 2) * DV_DF_W:(r // 2 + 1) * DV_DF_W] for r in range(n_maps)], axis=0)
        m_last, l_last, acc_last = m_ref[...], l_ref[...], acc_ref[...]
        mf = jnp.maximum(m_last, s_own)
        p_own = jnp.exp(s_own - mf)
        scale_last = jnp.exp(m_last - mf)
        lf = scale_last * l_last + p_own
        accf = scale_last * acc_last + p_own * v_own
        lam = _diff_lambda(lv_ref[...], lam_init)
        o_map = accf / lf
        for h in range(H_DF):
            o = o_map[2 * h:2 * h + 1, :] - lam * o_map[2 * h + 1:2 * h + 2, :]
            o_ref[:, h * DV_DF_W:(h + 1) * DV_DF_W] = (
                _rms(o, nw_ref[...]) * (1.0 - lam_init)).astype(BF16)


DV_DF_W = 2 * DH_DF


def _df_decode(dq, dk_new, dv_new, cache_k, cache_v, page_table, slopes8, lv, nw, layer,
               lam_init):
    b = dq.shape[0]
    n_pages = page_table.shape[1]
    kblock, vblock = cache_k.shape[2:], cache_v.shape[2:]
    assert n_pages % PAGES_PER_STEP == 0
    n_steps = n_pages // PAGES_PER_STEP
    tok = pl.BlockSpec((None, 1, BRANCH_W), lambda bi, g, pt: (bi, 0, 0))
    const = lambda shape: pl.BlockSpec(shape, lambda bi, g, pt: (0,) * len(shape))
    n_maps = 2 * H_DF
    return pl.pallas_call(
        functools.partial(_df_decode_kernel, n_steps=n_steps, lam_init=lam_init),
        out_shape=jax.ShapeDtypeStruct((b, 1, BRANCH_W), BF16),
        grid_spec=pltpu.PrefetchScalarGridSpec(
            num_scalar_prefetch=1, grid=(b, n_steps),
            in_specs=[tok, tok, tok, const(slopes8.shape), const(lv.shape), const(nw.shape)]
            + _page_specs(layer, n_pages, kblock, False)
            + _page_specs(layer, n_pages, vblock, False),
            out_specs=tok,
            scratch_shapes=[pltpu.VMEM(kblock, F32), pltpu.VMEM((n_maps, 1), F32),
                            pltpu.VMEM((n_maps, 1), F32), pltpu.VMEM((n_maps, DV_DF_W), F32)]),
        compiler_params=_params(("parallel", "arbitrary")),
        name="df_decode",
    )(page_table, dq, dk_new, dv_new, slopes8, lv, nw,
      *([cache_k] * PAGES_PER_STEP), *([cache_v] * PAGES_PER_STEP))


def _cache_views(cache_sb_k, cache_sb_v, cache_df_k, cache_df_v):
    d, n, p = cache_sb_k.shape[:3]
    keys_last = lambda c: jnp.transpose(
        c.reshape(d, n, p, BRANCH_W // DH_SB, DH_SB), (0, 1, 3, 4, 2))
    return (keys_last(cache_sb_k), keys_last(cache_sb_v), keys_last(cache_df_k),
            cache_df_v.reshape(d, n, p * H_DF // 8, 8, DV_DF_W))


def _pack_w_in(w):
    a = 9 * 512
    gab = w[:, a:a + 2 * H_GD]
    gz = w[:, a + 2 * H_GD:a + 2 * H_GD + 512]
    gates = w[:, a + 2 * H_GD + 512:]
    pad = jnp.zeros((w.shape[0], GAB_W - 2 * H_GD), w.dtype)
    return jnp.concatenate([w[:, :a], gz, gates, gab, pad], axis=1).astype(BF16)


def _lane_row(v):
    return jnp.pad(v.astype(F32), (0, LANES - v.shape[0]))[None, :]


def _row_tile(m):
    return min(m, 256)


def kernel(x_prompt, x_sample, cache_sb_k, cache_sb_v, cache_df_k, cache_df_v, state_gdn,
           state_gdn_conv, page_table, norm_w, ffn_w_in, ffn_w_out, w_in, diff_lambda,
           diff_norm_w, gdn_conv_w, gdn_a_log, gdn_dt_bias, gdn_norm_w, w_branch, w_out,
           final_norm_w):
    depth = w_in.shape[0]
    bp, tp, _ = x_prompt.shape
    bs, ts, _ = x_sample.shape
    assert ts == 1

    ffn_in = ffn_w_in.astype(BF16)
    ffn_out = ffn_w_out.astype(BF16)
    wb = w_branch.astype(BF16)
    wo = w_out.astype(BF16)
    fw = final_norm_w[None, :]
    slopes = 2.0 ** (-8.0 * jnp.arange(1, H_DF + 1, dtype=F32) / H_DF)
    slopes8 = jnp.broadcast_to(jnp.repeat(slopes, 2)[:, None], (2 * H_DF, LANES))
    ck, cv, dkc, dvc = _cache_views(cache_sb_k, cache_sb_v, cache_df_k, cache_df_v)

    xp = x_prompt.reshape(bp * tp, D_MODEL)
    xs = x_sample.reshape(bs, D_MODEL)
    tmp, tms = _row_tile(bp * tp), _row_tile(bs)
    p_states, s_states = [], []
    for l in range(depth):
        lam_init = 0.8 - 0.6 * math.exp(-0.3 * l)
        wl = _pack_w_in(w_in[l])
        nw = norm_w[l][:, None, :]
        prm = jnp.concatenate([_lane_row(gdn_a_log[l]), _lane_row(gdn_dt_bias[l]),
                               jnp.zeros((6, LANES), F32)], axis=0)
        gnw = gdn_norm_w[l][None, :]
        dnw = diff_norm_w[l][None, :]
        last = l == depth - 1

        xp = _ffn(xp, nw[0], ffn_in[l, 0], ffn_out[l, 0], fw, final=False, tm=tmp)
        sq, sk, sv, dq, dk, dv, xc, gz, gates, gab = _proj(xp, nw[1], wl, tm=tmp)
        r3 = lambda a: a.reshape(bp, tp, a.shape[-1])
        o_sb = _sb_prompt(r3(sq), r3(sk), r3(sv), tq=ATT_TQ)
        o_df = _df_prompt(r3(dq), r3(dk), r3(dv), slopes, diff_lambda[l], dnw, tq=ATT_TQ,
                          lam_init=lam_init)
        xc3 = r3(xc)
        o_gd, s_fin = _gdn_prompt(
            xc3, r3(gab), r3(gz), gdn_conv_w[l], prm, gnw,
            jnp.zeros((bp, 8, GD_CONV_CH), F32),
            jnp.zeros((bp, H_GD, DK_GD, DV_GD), F32))
        xp = _merge(xp, o_sb.reshape(bp * tp, BRANCH_W), o_df.reshape(bp * tp, BRANCH_W),
                    o_gd.reshape(bp * tp, BRANCH_W), gates, wb[l], wo[l], tm=tmp)
        xp = _ffn(xp, nw[2], ffn_in[l, 1], ffn_out[l, 1], fw, final=last, tm=tmp)
        p_states.append((
            sk.reshape(bp, tp, H_SB, DH_SB), sv.reshape(bp, tp, H_SB, DH_SB),
            dk.reshape(bp, tp, H_DF, 2, DH_DF), dv.reshape(bp, tp, H_DF, DV_DF_W),
            s_fin, xc3[:, tp - (CONV_W - 1):, :]))

        xs = _ffn(xs, nw[0], ffn_in[l, 0], ffn_out[l, 0], fw, final=False, tm=tms)
        sq, sk, sv, dq, dk, dv, xc, gz, gates, gab = _proj(xs, nw[1], wl, tm=tms)
        r3 = lambda a: a.reshape(bs, 1, a.shape[-1])
        o_sb = _sb_decode(r3(sq), ck, cv, page_table, l)
        o_df = _df_decode(r3(dq), r3(dk), r3(dv), dkc, dvc, page_table, slopes8,
                          diff_lambda[l], dnw, l, lam_init)
        o_gd, s_new = _gdn_step(r3(xc), r3(gab), r3(gz), gdn_conv_w[l], prm, gnw,
                                state_gdn_conv[l], state_gdn[l])
        xs = _merge(xs, o_sb.reshape(bs, BRANCH_W), o_df.reshape(bs, BRANCH_W),
                    o_gd.reshape(bs, BRANCH_W), gates, wb[l], wo[l], tm=tms)
        xs = _ffn(xs, nw[2], ffn_in[l, 1], ffn_out[l, 1], fw, final=last, tm=tms)
        new_buf = jnp.concatenate([state_gdn_conv[l][:, 1:, :], r3(xc)], axis=1)
        s_states.append((
            sk.reshape(bs, 1, H_SB, DH_SB), sv.reshape(bs, 1, H_SB, DH_SB),
            dk.reshape(bs, 1, H_DF, 2, DH_DF), dv.reshape(bs, 1, H_DF, DV_DF_W),
            s_new, new_buf))

    p_out = [jnp.stack(a) for a in zip(*p_states)]
    s_out = [jnp.stack(a) for a in zip(*s_states)]
    return (xp.reshape(bp, tp, D_MODEL), xs.reshape(bs, 1, D_MODEL), *p_out, *s_out)
```

```python
import functools
import math

import jax
import jax.numpy as jnp
from jax import lax
from jax.experimental import pallas as pl
from jax.experimental.pallas import tpu as pltpu

F32 = jnp.float32
BF16 = jnp.bfloat16

D_MODEL = 1024
DH_SB = 64
H_SB = 8
DH_DF = 64
H_DF = 4
DK_GD = 128
DV_GD = 128
H_GD = 4
CONV_W = 4
GD_CONV_CH = 2 * H_GD * DK_GD + H_GD * DV_GD
GDN_CHUNK = 64
GDN_CHUNKS_PER_STEP = 4
BRANCH_W = 512
N_BRANCH = 3
D_FF = 2816
NORM_EPS = 1e-6
L2_EPS = 1e-6
NEG_BIG = -1e30

LANES = 128
FF_CHUNK = 256
GAB_W = LANES
VMEM_LIMIT = 56 * 1024 * 1024

PROJ_OUTS = (
    ("sq", 512, BF16), ("sk", 512, F32), ("sv", 512, F32),
    ("dq", 512, BF16), ("dk", 512, F32), ("dv", 512, F32),
    ("xc", GD_CONV_CH, F32), ("gz", 512, F32),
    ("gates", N_BRANCH * D_MODEL, F32), ("gab", GAB_W, F32),
)


def _dot(a, b, precision=None):
    return jnp.dot(a, b, preferred_element_type=F32, precision=precision)


def _dot_nt(a, b, precision=None):
    return lax.dot_general(a, b, (((1,), (1,)), ((), ())),
                           preferred_element_type=F32, precision=precision)


def _dot_tn(a, b, precision=None):
    return lax.dot_general(a, b, (((0,), (0,)), ((), ())),
                           preferred_element_type=F32, precision=precision)


_HI = lax.Precision.HIGHEST


def _rms(x, w):
    ms = jnp.mean(x * x, axis=-1, keepdims=True)
    return x * lax.rsqrt(ms + NORM_EPS) * w


def _sigmoid(x):
    return 1.0 / (1.0 + jnp.exp(-x))


def _silu(x):
    return x * _sigmoid(x)


def _softplus(x):
    return jnp.maximum(x, 0.0) + jnp.log(1.0 + jnp.exp(-jnp.abs(x)))


def _params(sem):
    return pltpu.CompilerParams(dimension_semantics=sem, vmem_limit_bytes=VMEM_LIMIT)


def _resident(shape):
    nd = len(shape)
    return pl.BlockSpec(shape, lambda *_: (0,) * nd, pipeline_mode=pl.Buffered(1))


def _ffn_kernel(x_ref, nw_ref, win_ref, wout_ref, fw_ref, o_ref, *, final):
    x = x_ref[...]
    h = _rms(x, nw_ref[...]).astype(BF16)
    acc = jnp.zeros(x.shape, F32)
    for c in range(D_FF // FF_CHUNK):
        lo = c * FF_CHUNK
        g = _dot(h, win_ref[:, lo:lo + FF_CHUNK])
        u = _dot(h, win_ref[:, D_FF + lo:D_FF + lo + FF_CHUNK])
        a = (_silu(g) * u).astype(BF16)
        acc = acc + _dot(a, wout_ref[lo:lo + FF_CHUNK, :])
    y = x + 0.5 * acc
    if final:
        y = _rms(y, fw_ref[...])
    o_ref[...] = y


def _ffn(x, nw, w_in, w_out, fw, *, final, tm):
    m = x.shape[0]
    assert m % tm == 0
    row = pl.BlockSpec((tm, D_MODEL), lambda i: (i, 0))
    return pl.pallas_call(
        functools.partial(_ffn_kernel, final=final),
        out_shape=jax.ShapeDtypeStruct((m, D_MODEL), F32),
        grid=(m // tm,),
        in_specs=[row, _resident((1, D_MODEL)), _resident(w_in.shape),
                  _resident(w_out.shape), _resident((1, D_MODEL))],
        out_specs=row,
        compiler_params=_params(("parallel",)),
        name="ffn",
    )(x, nw, w_in, w_out, fw)


W_IN_ROWS = {"sq": (0, 512), "sk": (512, 1024), "sv": (1024, 1536), "dq": (1536, 2048),
             "dk": (2048, 2560), "dv": (2560, 3072), "xc": (3072, 4608), "gab": (4608, 4616),
             "gz": (4616, 5128), "gates": (5128, 8200)}
KV_OUTS = ("sk", "sv", "dk", "dv")
FEATURE_MAJOR = ("sk", "sv", "dk")


def _proj_kernel(x_ref, nw_ref, wt_ref, *refs, feature_major):
    o_refs = refs[len(refs) - len(PROJ_OUTS):]
    h = _rms(x_ref[...], nw_ref[...]).astype(BF16)
    for (name, width, dt), o_ref in zip(PROJ_OUTS, o_refs):
        r0, r1 = W_IN_ROWS[name]
        if name in feature_major:
            o_ref[...] = _dot_nt(wt_ref[r0:r1, :], h).astype(dt)
        elif name == "gab":
            g = _dot_nt(h, wt_ref[r0:r1, :])
            o_ref[...] = jnp.concatenate(
                [g, jnp.zeros((g.shape[0], width - g.shape[1]), F32)], axis=1)
        else:
            for lo in range(0, width, 512):
                o_ref[:, lo:lo + 512] = _dot_nt(h, wt_ref[r0 + lo:r0 + lo + 512, :]).astype(dt)


def _proj(x, nw, wt, *, tm, stack=None):
    m = x.shape[0]
    assert m % tm == 0
    shapes, specs = [], []
    for name, wd, dt in PROJ_OUTS:
        if stack is None or name not in KV_OUTS:
            shapes.append(jax.ShapeDtypeStruct((m, wd), dt))
            specs.append(pl.BlockSpec((tm, wd), lambda i: (i, 0)))
            continue
        layer, depth, batch, _ = stack
        steps = m // batch // tm
        if name in FEATURE_MAJOR:
            shapes.append(jax.ShapeDtypeStruct((depth, batch, wd, m // batch), dt))
            specs.append(pl.BlockSpec((None, None, wd, tm),
                                      lambda i, layer=layer: (layer, i // steps, 0, i % steps)))
        else:
            shapes.append(jax.ShapeDtypeStruct((depth, m, wd), dt))
            specs.append(pl.BlockSpec((None, tm, wd), lambda i, layer=layer: (layer, i, 0)))
    prev = [] if stack is None or stack[3] is None else list(stack[3])
    kv_index = [i for i, (name, _, _) in enumerate(PROJ_OUTS) if name in KV_OUTS]
    return pl.pallas_call(
        functools.partial(_proj_kernel,
                          feature_major=FEATURE_MAJOR if stack is not None else ()),
        out_shape=shapes,
        grid=(m // tm,),
        in_specs=[pl.BlockSpec((tm, D_MODEL), lambda i: (i, 0)),
                  _resident((1, D_MODEL)), _resident(wt.shape)]
        + [pl.BlockSpec(memory_space=pl.ANY)] * len(prev),
        out_specs=specs,
        input_output_aliases={3 + j: kv_index[j] for j in range(len(prev))},
        compiler_params=_params(("parallel",)),
        name="proj",
    )(x, nw, wt, *prev)


def _merge_kernel(x_ref, osb_ref, odf_ref, ogd_ref, gates_ref, wb_ref, wo_ref, o_ref):
    m = None
    for i, b_ref in enumerate((osb_ref, odf_ref, ogd_ref)):
        p = _dot(b_ref[...], wb_ref[i])
        t = _sigmoid(gates_ref[:, i * D_MODEL:(i + 1) * D_MODEL]) * p
        m = t if m is None else m + t
    o_ref[...] = x_ref[...] + _dot(m.astype(BF16), wo_ref[...])


def _merge(x, o_sb, o_df, o_gd, gates, wb, wo, *, tm):
    m = x.shape[0]
    assert m % tm == 0
    row = lambda w: pl.BlockSpec((tm, w), lambda i: (i, 0))
    return pl.pallas_call(
        _merge_kernel,
        out_shape=jax.ShapeDtypeStruct((m, D_MODEL), F32),
        grid=(m // tm,),
        in_specs=[row(D_MODEL), row(BRANCH_W), row(BRANCH_W), row(BRANCH_W),
                  row(N_BRANCH * D_MODEL), _resident(wb.shape), _resident(wo.shape)],
        out_specs=row(D_MODEL),
        compiler_params=_params(("parallel",)),
        name="merge",
    )(x, o_sb, o_df, o_gd, gates, wb, wo)


def _log_sigmoid_pair(z):
    ls = jnp.minimum(z, 0.0) - jnp.log(1.0 + jnp.exp(-jnp.abs(z)))
    return ls, ls - z


def _suffix_sums(x, upper):
    hi = x.astype(BF16)
    lo = (x - hi.astype(F32)).astype(BF16)
    return _dot(hi, upper) + _dot(lo, upper)


ATT_TK = 256
ATT_TQ = 512
DF_BLOCKS_PER_TRIP = 4
SB_DONE = -110.0


def _stack_halves(q, seg):
    lane = lax.broadcasted_iota(jnp.int32, q.shape, 1)
    return jnp.concatenate(
        [jnp.where(lane // seg == j, q, 0.0) for j in range(LANES // seg)], axis=0).astype(BF16)


def _walk_key_blocks(qi, tq, blocks, state, per_trip):
    nd = tq // ATT_TK
    state = blocks([(qi + 1) * nd - 1 - d for d in range(nd)], state, True)
    n_free = qi * nd

    def full_trip(i, st):
        kj = n_free - 1 - per_trip * i
        return blocks([kj - d for d in range(per_trip)], st, False)

    def last_one(i, st):
        return blocks([n_free % per_trip - 1 - i], st, False)

    state = lax.fori_loop(0, n_free // per_trip, full_trip, state)
    return lax.fori_loop(0, n_free % per_trip, last_one, state)


def _sb_prompt_kernel(q_ref, k_ref, v_ref, o_ref, kb_ref, vb_ref, *, tq):
    qi = pl.program_id(2)
    tk = ATT_TK

    @pl.when(qi == 0)
    def _():
        for j in range(kb_ref.shape[0]):
            kb_ref[j] = k_ref[:, j * tk:(j + 1) * tk].astype(BF16)
            vb_ref[j] = v_ref[:, j * tk:(j + 1) * tk].astype(BF16)

    qs = _stack_halves(q_ref[...].astype(F32) * (DH_SB ** -0.5), DH_SB)
    rows = qs.shape[0]
    rowq = lax.broadcasted_iota(jnp.int32, (rows, tk), 0) % tq
    col = lax.broadcasted_iota(jnp.int32, (rows, tk), 1)
    ur = lax.broadcasted_iota(jnp.int32, (tk, tk), 0)
    uc = lax.broadcasted_iota(jnp.int32, (tk, tk), 1)
    upper = jnp.where(ur > uc, 1.0, 0.0).astype(BF16)

    def blocks(kjs, state, masked):
        carry, acc = state
        pairs = [_log_sigmoid_pair(_dot(qs, kb_ref[kj])) for kj in kjs]
        lss = [p[0] for p in pairs]
        l1ms = [p[1] for p in pairs]
        if masked:
            befores = [col + (kj * tk - qi * tq) < rowq for kj in kjs]
            l1ms = [jnp.where(b, x, 0.0) for b, x in zip(befores, l1ms)]
        sufs = [_suffix_sums(x, upper) for x in l1ms]
        carries = [carry]
        for x in l1ms:
            carries.append(carries[-1] + jnp.sum(x, axis=-1, keepdims=True))
        probs = [jnp.exp(ls + suf + c) for ls, suf, c in zip(lss, sufs, carries)]
        if masked:
            probs = [jnp.where(b, a, 0.0) for b, a in zip(befores, probs)]
        for a, kj in zip(probs, kjs):
            acc = acc + _dot_nt(a.astype(BF16), vb_ref[kj])
        return carries[-1], acc

    nd = tq // tk
    state = blocks([(qi + 1) * nd - 1 - d for d in range(nd)],
                   (jnp.zeros((rows, 1), F32), jnp.zeros((rows, LANES), F32)), True)

    def unfinished(loop):
        kj, carry, _ = loop
        return jnp.logical_and(kj >= 0, jnp.max(carry) > SB_DONE)

    def one_block(loop):
        kj, carry, acc = loop
        carry, acc = blocks([kj], (carry, acc), False)
        return kj - 1, carry, acc

    _, _, acc = lax.while_loop(unfinished, one_block, (qi * nd - 1,) + state)
    lane = lax.broadcasted_iota(jnp.int32, (tq, LANES), 1)
    o_ref[...] = jnp.where(lane < DH_SB, acc[:tq], acc[tq:]).astype(BF16)


def _sb_prompt(sq, sk_t, sv_t, layer, *, tq):
    b, t, _ = sq.shape
    assert tq % ATT_TK == 0 and t % tq == 0
    qspec = pl.BlockSpec((None, tq, LANES), lambda bi, hp, qi: (bi, qi, hp))
    kspec = pl.BlockSpec((None, None, LANES, t), lambda bi, hp, qi: (layer, bi, hp, 0))
    return pl.pallas_call(
        functools.partial(_sb_prompt_kernel, tq=tq),
        out_shape=jax.ShapeDtypeStruct((b, t, BRANCH_W), BF16),
        grid=(b, BRANCH_W // LANES, t // tq),
        in_specs=[qspec, kspec, kspec],
        out_specs=qspec,
        scratch_shapes=[pltpu.VMEM((t // ATT_TK, LANES, ATT_TK), BF16)] * 2,
        compiler_params=_params(("parallel", "parallel", "arbitrary")),
        name="sb_prompt",
    )(sq, sk_t, sv_t)


def _diff_lambda(lv, lam_init):
    a = jnp.sum(lv[0:1] * lv[1:2], axis=-1, keepdims=True)
    b = jnp.sum(lv[2:3] * lv[3:4], axis=-1, keepdims=True)
    return jnp.exp(a) - jnp.exp(b) + lam_init


def _df_prompt_kernel(slope_ref, q_ref, k_ref, v_ref, lv_ref, nw_ref, o_ref, kb_ref, vb_ref,
                      *, tq, lam_init):
    h = pl.program_id(1)
    qi = pl.program_id(2)

    tk = ATT_TK
    slope = slope_ref[h]

    @pl.when(qi == 0)
    def _():
        vb_ref[...] = v_ref[...].astype(BF16)
        lane = lax.broadcasted_iota(jnp.int32, (LANES, tk), 1)
        feat = lax.broadcasted_iota(jnp.int32, (LANES, tk), 0)
        for j in range(kb_ref.shape[0]):
            kb_ref[j, 0:LANES, :] = k_ref[:, j * tk:(j + 1) * tk].astype(BF16)
            hi, lo = _two_terms((lane + j * tk).astype(F32) * slope)
            kb_ref[j, LANES:2 * LANES, :] = jnp.where(
                feat == 0, hi, jnp.where(feat == 1, lo, jnp.where(feat == 2, 1.0, 0.0).astype(BF16)))

    qs = _stack_halves(q_ref[...].astype(F32) * (DH_DF ** -0.5), DH_DF)
    rows = qs.shape[0]
    qlane = lax.broadcasted_iota(jnp.int32, (rows, LANES), 1)
    q_bias = jnp.where(qlane < 2, 1.0, jnp.where(qlane == 2, -slope * (qi * tq).astype(F32), 0.0))
    qs = jnp.concatenate([qs, q_bias.astype(BF16)], axis=1)
    rowq = lax.broadcasted_iota(jnp.int32, (rows, tk), 0) % tq
    col = lax.broadcasted_iota(jnp.int32, (rows, tk), 1)

    def block(kj, state, masked):
        m, l, acc = state
        start = pl.multiple_of(kj * tk, tk)
        offset = qi * tq - kj * tk
        s = _dot(qs, kb_ref[kj])
        if masked:
            visible = col - offset <= rowq
            s = jnp.where(visible, s, NEG_BIG)
        m_new = jnp.maximum(m, jnp.max(s, axis=-1, keepdims=True))
        p = jnp.exp(s - m_new)
        if masked:
            p = jnp.where(visible, p, 0.0)
        alpha = jnp.exp(m - m_new)
        l = alpha * l + jnp.sum(p, axis=-1, keepdims=True)
        acc = alpha * acc + _dot(p.astype(BF16), vb_ref[pl.ds(start, tk), :])
        return m_new, l, acc

    def blocks(kjs, state, masked):
        for kj in kjs:
            state = block(kj, state, masked)
        return state

    _, l, acc = _walk_key_blocks(
        qi, tq, blocks, (jnp.full((rows, 1), NEG_BIG, F32), jnp.zeros((rows, 1), F32),
                         jnp.zeros((rows, LANES), F32)), DF_BLOCKS_PER_TRIP)
    lam = _diff_lambda(lv_ref[...], lam_init)
    o = acc[:tq] / l[:tq] - lam * (acc[tq:] / l[tq:])
    o_ref[...] = (_rms(o, nw_ref[...]) * (1.0 - lam_init)).astype(BF16)


def _df_prompt(dq, dk_t, dv, layer, slopes, lv, nw, *, tq, lam_init):
    b, t, _ = dq.shape
    assert tq % ATT_TK == 0 and t % tq == 0
    qspec = pl.BlockSpec((None, tq, LANES), lambda bi, h, qi: (bi, qi, h))
    kspec = pl.BlockSpec((None, None, LANES, t), lambda bi, h, qi: (layer, bi, h, 0))
    vspec = pl.BlockSpec((None, None, t, LANES), lambda bi, h, qi: (layer, bi, 0, h))
    return pl.pallas_call(
        functools.partial(_df_prompt_kernel, tq=tq, lam_init=lam_init),
        out_shape=jax.ShapeDtypeStruct((b, t, BRANCH_W), BF16),
        grid=(b, H_DF, t // tq),
        in_specs=[pl.BlockSpec(memory_space=pltpu.SMEM), qspec, kspec, vspec,
                  _resident(lv.shape), _resident(nw.shape)],
        out_specs=qspec,
        scratch_shapes=[pltpu.VMEM((t // ATT_TK, 2 * LANES, ATT_TK), BF16),
                        pltpu.VMEM((t, LANES), BF16)],
        compiler_params=_params(("parallel", "parallel", "arbitrary")),
        name="df_prompt",
    )(slopes, dq, dk_t, dv, lv, nw)


def _gdn_gates(gab, prm):
    g = -jnp.exp(prm[0:1, :]) * _softplus(gab + prm[1:2, :])
    return g, _sigmoid(gab)


def _l2n(x):
    return x * lax.rsqrt(jnp.sum(x * x, axis=-1, keepdims=True) + L2_EPS)


def _two_terms(x):
    hi = x.astype(BF16)
    return hi, (x - hi.astype(F32)).astype(BF16)


def _split_lhs(a):
    hi, lo = _two_terms(a)
    return jnp.concatenate([hi, hi, lo], axis=1)


def _split_rhs(b):
    hi, lo = _two_terms(b)
    return jnp.concatenate([hi, lo, hi], axis=0)


def _gdn_out(o, nw, z):
    return (_rms(o, nw) * _silu(z)).astype(BF16)


def _gdn_prompt_kernel(xc_ref, gab_ref, gz_ref, cw_ref, prm_ref, nw_ref, cb_ref, s0_ref,
                       o_ref, sfin_ref, buf_ref, st_ref, *, n_steps):
    c = pl.program_id(1)
    ch = GDN_CHUNK
    rows = GDN_CHUNKS_PER_STEP * ch

    @pl.when(c == 0)
    def _():
        buf_ref[0:8, :] = cb_ref[...]
        st_ref[...] = s0_ref[...]

    buf_ref[8:8 + rows, :] = xc_ref[...]
    xconv = None
    for j in range(CONV_W):
        t = cw_ref[j:j + 1, :] * buf_ref[5 + j:5 + j + rows, :]
        xconv = t if xconv is None else xconv + t
    buf_ref[0:8, :] = buf_ref[rows:rows + 8, :]
    xs = _silu(xconv)

    g_all, beta_all = _gdn_gates(gab_ref[...], prm_ref[...])
    row = lax.broadcasted_iota(jnp.int32, (ch, ch), 0)
    col = lax.broadcasted_iota(jnp.int32, (ch, ch), 1)
    tri = col <= row
    strict = col < row
    eye = jnp.where(row == col, 1.0, 0.0)
    lower = jnp.where(tri, 1.0, 0.0)
    hk = H_GD * DK_GD

    problems = [(j, h) for j in range(GDN_CHUNKS_PER_STEP) for h in range(H_GD)]
    cums = []
    for j in range(GDN_CHUNKS_PER_STEP):
        cum_all = _dot(lower, g_all[j * ch:(j + 1) * ch], _HI)
        cums.append((cum_all, cum_all.T))

    def first_stage(j, h):
        r0 = j * ch
        q = _l2n(xs[r0:r0 + ch, h * DK_GD:(h + 1) * DK_GD]) * (DK_GD ** -0.5)
        k = _l2n(xs[r0:r0 + ch, hk + h * DK_GD:hk + (h + 1) * DK_GD])
        v = xs[r0:r0 + ch, 2 * hk + h * DV_GD:2 * hk + (h + 1) * DV_GD]
        gc = cums[j][0][:, h:h + 1]
        gr = cums[j][1][h:h + 1, :]
        beta = beta_all[r0:r0 + ch, H_GD + h:H_GD + h + 1]
        decay = jnp.exp(jnp.where(tri, gc - gr, NEG_BIG))
        eg = jnp.exp(gc)
        g_last = gc[ch - 1:ch, :]
        kb = k.astype(BF16)
        qk = (_dot_nt(q.astype(BF16), kb) * decay).astype(BF16)
        pw = -jnp.where(strict, beta * decay * _dot_nt(kb, kb), 0.0)
        rhs = _split_rhs(jnp.concatenate([beta * v, (beta * eg) * k], axis=1))
        carry_over = ((q * eg).astype(BF16), qk, (k * jnp.exp(g_last - gc)).astype(BF16),
                      jnp.exp(g_last))
        return pw, rhs, carry_over

    staged = [first_stage(j, h) for j, h in problems]
    pws = [p[0] for p in staged]
    invs = [eye + pw for pw in pws]
    for _ in range(5):
        pws = [_dot(_split_lhs(pw), _split_rhs(pw)) for pw in pws]
        invs = [inv + _dot(_split_lhs(inv), _split_rhs(pw)) for inv, pw in zip(invs, pws)]
    sols = [_dot(_split_lhs(inv), p[1]) for inv, p in zip(invs, staged)]
    prepared = {jh: (sol[:, :DV_GD], sol[:, DV_GD:].astype(BF16)) + p[2]
                for jh, sol, p in zip(problems, sols, staged)}

    states = [st_ref[h] for h in range(H_GD)]
    for j in range(GDN_CHUNKS_PER_STEP):
        for h in range(H_GD):
            sol_v, sol_k, q_eg, qk, k_dec, e_last = prepared[(j, h)]
            s = states[h]
            sb = s.astype(BF16)
            ub = (sol_v - _dot(sol_k, sb)).astype(BF16)
            o = _dot(q_eg, sb) + _dot(qk, ub)
            states[h] = e_last * s + _dot_tn(k_dec, ub)
            o_ref[j * ch:(j + 1) * ch, h * DV_GD:(h + 1) * DV_GD] = _gdn_out(
                o, nw_ref[...], gz_ref[j * ch:(j + 1) * ch, h * DV_GD:(h + 1) * DV_GD])
    for h in range(H_GD):
        st_ref[h] = states[h]

    @pl.when(c == n_steps - 1)
    def _():
        sfin_ref[...] = st_ref[...]


def _gdn_prompt(xc, gab, gz, cw, prm, nw, cb8, s0):
    b, t, _ = xc.shape
    rows = GDN_CHUNKS_PER_STEP * GDN_CHUNK
    assert t % rows == 0
    n_steps = t // rows
    tok = lambda w: pl.BlockSpec((None, rows, w), lambda bi, c: (bi, c, 0))
    st = pl.BlockSpec((None, H_GD, DK_GD, DV_GD), lambda bi, c: (bi, 0, 0, 0))
    return pl.pallas_call(
        functools.partial(_gdn_prompt_kernel, n_steps=n_steps),
        out_shape=[jax.ShapeDtypeStruct((b, t, BRANCH_W), BF16),
                   jax.ShapeDtypeStruct((b, H_GD, DK_GD, DV_GD), F32)],
        grid=(b, n_steps),
        in_specs=[tok(GD_CONV_CH), tok(GAB_W), tok(BRANCH_W), _resident(cw.shape),
                  _resident(prm.shape), _resident(nw.shape),
                  pl.BlockSpec((None, 8, GD_CONV_CH), lambda bi, c: (bi, 0, 0)), st],
        out_specs=[tok(BRANCH_W), st],
        scratch_shapes=[pltpu.VMEM((rows + 8, GD_CONV_CH), F32),
                        pltpu.VMEM((H_GD, DK_GD, DV_GD), F32)],
        compiler_params=_params(("parallel", "arbitrary")),
        name="gdn_prompt",
    )(xc, gab, gz, cw, prm, nw, cb8, s0)


def _gdn_step_kernel(xc_ref, gab_ref, gz_ref, cw_ref, prm_ref, nw_ref, cb_ref, s0_ref,
                     o_ref, sn_ref):
    xconv = cw_ref[CONV_W - 1:CONV_W, :] * xc_ref[...]
    for j in range(CONV_W - 1):
        xconv = xconv + cw_ref[j:j + 1, :] * cb_ref[j:j + 1, :]
    xs = _silu(xconv)
    g_all, beta_all = _gdn_gates(gab_ref[...], prm_ref[...])
    first = lax.broadcasted_iota(jnp.int32, (8, DK_GD), 0) == 0
    rows8 = lambda x: jnp.where(first, jnp.broadcast_to(x, (8, x.shape[-1])), 0.0)
    hk = H_GD * DK_GD
    for h in range(H_GD):
        q = rows8(_l2n(xs[:, h * DK_GD:(h + 1) * DK_GD]) * (DK_GD ** -0.5))
        k = rows8(_l2n(xs[:, hk + h * DK_GD:hk + (h + 1) * DK_GD]))
        v = xs[:, 2 * hk + h * DV_GD:2 * hk + (h + 1) * DV_GD]
        eg = jnp.exp(g_all[:, h:h + 1])
        beta = beta_all[:, H_GD + h:H_GD + h + 1]
        s = s0_ref[h]
        u = beta * (v - eg * _dot(k, s, _HI))
        s_new = eg * s + _dot_tn(k, u, _HI)
        sn_ref[h] = s_new
        o = _dot(q, s_new, _HI)[0:1, :]
        o_ref[:, h * DV_GD:(h + 1) * DV_GD] = _gdn_out(
            o, nw_ref[...], gz_ref[:, h * DV_GD:(h + 1) * DV_GD])


def _gdn_step(xc, gab, gz, cw, prm, nw, cb, s0):
    b = xc.shape[0]
    tok = lambda w: pl.BlockSpec((None, 1, w), lambda bi: (bi, 0, 0))
    st = pl.BlockSpec((None, H_GD, DK_GD, DV_GD), lambda bi: (bi, 0, 0, 0))
    return pl.pallas_call(
        _gdn_step_kernel,
        out_shape=[jax.ShapeDtypeStruct((b, 1, BRANCH_W), BF16),
                   jax.ShapeDtypeStruct((b, H_GD, DK_GD, DV_GD), F32)],
        grid=(b,),
        in_specs=[tok(GD_CONV_CH), tok(GAB_W), tok(BRANCH_W), _resident(cw.shape),
                  _resident(prm.shape), _resident(nw.shape),
                  pl.BlockSpec((None, CONV_W - 1, GD_CONV_CH), lambda bi: (bi, 0, 0)), st],
        out_specs=[tok(BRANCH_W), st],
        compiler_params=_params(("parallel",)),
        name="gdn_step",
    )(xc, gab, gz, cw, prm, nw, cb, s0)


PAGES_PER_STEP = 8


def _page_specs(layer, n_pages, block, descending):
    specs = []
    for i in range(PAGES_PER_STEP):
        def index(b, g, pt, i=i):
            p = g * PAGES_PER_STEP + i
            if descending:
                p = n_pages - 1 - p
            return (layer, pt[b, p]) + (0,) * len(block)
        specs.append(pl.BlockSpec((None, None) + block, index))
    return specs


def _store_lane_columns(q, qc_ref):
    n_seg, seg, lanes = qc_ref.shape
    first = lax.broadcasted_iota(jnp.int32, (8, seg), 0) == 0
    ones = jnp.where(lax.broadcasted_iota(jnp.int32, (8, lanes), 0) == 0, 1.0, 0.0).astype(BF16)
    for s in range(n_seg):
        qs = jnp.where(first, jnp.broadcast_to(q[:, s * seg:(s + 1) * seg], (8, seg)), 0.0)
        qc_ref[s] = _dot_tn(qs.astype(BF16), ones)


def _page_scores(k_ref, qc_ref):
    return jnp.concatenate(
        [jnp.sum(k_ref[s] * qc_ref[s], axis=0, keepdims=True) for s in range(k_ref.shape[0])],
        axis=0)


def _head_rows(q, n_rows, seg):
    w = q.shape[-1]
    r = lax.broadcasted_iota(jnp.int32, (n_rows, w), 0)
    c = lax.broadcasted_iota(jnp.int32, (n_rows, w), 1)
    return jnp.where(c // seg == r, jnp.broadcast_to(q, (n_rows, w)), 0.0)


SB_PAGES_PER_GROUP = 2


def _sb_decode_kernel(pt_ref, q_ref, k_hbm, v_hbm, o_ref, kbuf, vbuf, sems, qc_ref, acc_ref,
                      *, layer, n_pages):
    b = pl.program_id(0)
    grp = SB_PAGES_PER_GROUP
    n_groups = n_pages // grp
    page_rows = qc_ref.shape[-1]

    def copies(g):
        slot = g % 2
        out = []
        for i in range(grp):
            page = pt_ref[b, n_pages - 1 - (g * grp + i)]
            out.append(pltpu.make_async_copy(k_hbm.at[layer, page], kbuf.at[slot, i], sems.at[slot, 0, i]))
            out.append(pltpu.make_async_copy(v_hbm.at[layer, page], vbuf.at[slot, i], sems.at[slot, 1, i]))
        return out

    for c in copies(0):
        c.start()
    _store_lane_columns(q_ref[...].astype(F32) * (DH_SB ** -0.5), qc_ref)
    acc_ref[...] = jnp.zeros(acc_ref.shape, F32)
    row = lax.broadcasted_iota(jnp.int32, (page_rows, page_rows), 0)
    col = lax.broadcasted_iota(jnp.int32, (page_rows, page_rows), 1)
    upper = jnp.where(row > col, 1.0, 0.0).astype(BF16)

    def unfinished(loop):
        g, carry = loop
        return jnp.logical_and(g < n_groups, jnp.max(carry) > SB_DONE)

    def one_group(loop):
        g, carry = loop
        slot = g % 2

        @pl.when(g + 1 < n_groups)
        def _():
            for c in copies(g + 1):
                c.start()

        for c in copies(g):
            c.wait()
        z = jnp.concatenate([_page_scores(kbuf.at[slot, i], qc_ref) for i in range(grp)], axis=0)
        ls, l1m = _log_sigmoid_pair(z)
        page_sums = jnp.sum(l1m, axis=-1, keepdims=True)
        carries = [carry]
        for i in range(grp):
            carries.append(carries[-1] + page_sums[i * H_SB:(i + 1) * H_SB])
        a = jnp.exp(ls + _suffix_sums(l1m, upper) + jnp.concatenate(carries[:-1], axis=0))
        for h in range(H_SB):
            t = acc_ref[h]
            for i in range(grp):
                t = t + vbuf[slot, i, h] * a[i * H_SB + h:i * H_SB + h + 1, :]
            acc_ref[h] = t
        return g + 1, carries[-1]

    g_end, _ = lax.while_loop(unfinished, one_group, (0, jnp.zeros((H_SB, 1), F32)))

    @pl.when(g_end < n_groups)
    def _():
        for c in copies(g_end):
            c.wait()

    ones = jnp.ones((8, page_rows), F32)
    o_ref[...] = jnp.concatenate(
        [_dot_nt(ones, acc_ref[h], _HI)[0:1, :] for h in range(H_SB)], axis=1).astype(BF16)


def _sb_decode(sq, cache_k, cache_v, page_table, layer):
    b = sq.shape[0]
    n_pages = page_table.shape[1]
    block = cache_k.shape[2:]
    grp = SB_PAGES_PER_GROUP
    assert n_pages % grp == 0
    tok = pl.BlockSpec((None, 1, BRANCH_W), lambda bi, pt: (bi, 0, 0))
    hbm = pl.BlockSpec(memory_space=pl.ANY)
    return pl.pallas_call(
        functools.partial(_sb_decode_kernel, layer=layer, n_pages=n_pages),
        out_shape=jax.ShapeDtypeStruct((b, 1, BRANCH_W), BF16),
        grid_spec=pltpu.PrefetchScalarGridSpec(
            num_scalar_prefetch=1, grid=(b,),
            in_specs=[tok, hbm, hbm], out_specs=tok,
            scratch_shapes=[pltpu.VMEM((2, grp) + block, F32), pltpu.VMEM((2, grp) + block, F32),
                            pltpu.SemaphoreType.DMA((2, 2, grp)),
                            pltpu.VMEM(block, F32), pltpu.VMEM(block, F32)]),
        compiler_params=_params(("arbitrary",)),
        name="sb_decode",
    )(page_table, sq, cache_k, cache_v)


def _df_decode_kernel(pt_ref, q_ref, kn_ref, vn_ref, slope_ref, lv_ref, nw_ref, *refs,
                      n_steps, lam_init):
    k_refs = refs[:PAGES_PER_STEP]
    v_refs = refs[PAGES_PER_STEP:2 * PAGES_PER_STEP]
    o_ref, qc_ref, m_ref, l_ref, acc_ref = refs[2 * PAGES_PER_STEP:]
    g = pl.program_id(1)
    n_maps = 2 * H_DF
    page_rows = qc_ref.shape[-1]
    past = n_steps * PAGES_PER_STEP * page_rows
    q = q_ref[...].astype(F32) * (DH_DF ** -0.5)

    @pl.when(g == 0)
    def _():
        _store_lane_columns(q, qc_ref)
        m_ref[...] = jnp.full(m_ref.shape, NEG_BIG, F32)
        l_ref[...] = jnp.zeros(l_ref.shape, F32)
        acc_ref[...] = jnp.zeros(acc_ref.shape, F32)

    rows = PAGES_PER_STEP * n_maps
    slope = jnp.concatenate([slope_ref[...][:, 0:1]] * PAGES_PER_STEP, axis=0)
    page_of_row = lax.broadcasted_iota(jnp.int32, (rows, page_rows), 0) // n_maps
    within = lax.broadcasted_iota(jnp.int32, (rows, page_rows), 1)
    first_key = (g * PAGES_PER_STEP + page_of_row) * page_rows
    dist = (past - first_key - within).astype(F32)
    flat = H_DF * page_rows
    ek = lax.broadcasted_iota(jnp.int32, (page_rows, flat), 0)
    er = lax.broadcasted_iota(jnp.int32, (page_rows, flat), 1)
    expand = jnp.where(er // H_DF == ek, 1.0, 0.0).astype(BF16)
    mr = lax.broadcasted_iota(jnp.int32, (rows, flat), 0)
    mc = lax.broadcasted_iota(jnp.int32, (rows, flat), 1)
    own_head = mc % H_DF == (mr % n_maps) // 2
    m, l, acc = m_ref[...], l_ref[...], acc_ref[...]

    def over_pages(op, x):
        return functools.reduce(op, [x[i * n_maps:(i + 1) * n_maps] for i in range(PAGES_PER_STEP)])

    s = jnp.concatenate([_page_scores(k_ref, qc_ref) for k_ref in k_refs], axis=0) - slope * dist
    m_new = jnp.maximum(m, over_pages(jnp.maximum, jnp.max(s, axis=-1, keepdims=True)))
    p = jnp.exp(s - jnp.concatenate([m_new] * PAGES_PER_STEP, axis=0))
    alpha = jnp.exp(m - m_new)
    pe = jnp.where(own_head, _dot(p.astype(BF16), expand), 0.0)
    acc = alpha * acc
    for i, v_ref in enumerate(v_refs):
        acc = acc + _dot(pe[i * n_maps:(i + 1) * n_maps].astype(BF16),
                         v_ref[...].reshape(flat, DV_DF_W).astype(BF16))
    m_ref[...] = m_new
    l_ref[...] = alpha * l + over_pages(jnp.add, jnp.sum(p, axis=-1, keepdims=True))
    acc_ref[...] = acc

    @pl.when(g == n_steps - 1)
    def _():
        s_own = jnp.sum(_head_rows(q, n_maps, DH_DF) * kn_ref[...], axis=-1, keepdims=True)
        v_own = jnp.concatenate(
            [vn_ref[:, (r // 2) * DV_DF_W:(r // 2 + 1) * DV_DF_W] for r in range(n_maps)], axis=0)
        m_last, l_last, acc_last = m_ref[...], l_ref[...], acc_ref[...]
        mf = jnp.maximum(m_last, s_own)
        p_own = jnp.exp(s_own - mf)
        scale_last = jnp.exp(m_last - mf)
        lf = scale_last * l_last + p_own
        accf = scale_last * acc_last + p_own * v_own
        lam = _diff_lambda(lv_ref[...], lam_init)
        o_map = accf / lf
        for h in range(H_DF):
            o = o_map[2 * h:2 * h + 1, :] - lam * o_map[2 * h + 1:2 * h + 2, :]
            o_ref[:, h * DV_DF_W:(h + 1) * DV_DF_W] = (
                _rms(o, nw_ref[...]) * (1.0 - lam_init)).astype(BF16)


DV_DF_W = 2 * DH_DF


def _df_decode(dq, dk_new, dv_new, cache_k, cache_v, page_table, slopes8, lv, nw, layer,
               lam_init):
    b = dq.shape[0]
    n_pages = page_table.shape[1]
    kblock, vblock = cache_k.shape[2:], cache_v.shape[2:]
    assert n_pages % PAGES_PER_STEP == 0
    n_steps = n_pages // PAGES_PER_STEP
    tok = pl.BlockSpec((None, 1, BRANCH_W), lambda bi, g, pt: (bi, 0, 0))
    const = lambda shape: pl.BlockSpec(shape, lambda bi, g, pt: (0,) * len(shape))
    n_maps = 2 * H_DF
    return pl.pallas_call(
        functools.partial(_df_decode_kernel, n_steps=n_steps, lam_init=lam_init),
        out_shape=jax.ShapeDtypeStruct((b, 1, BRANCH_W), BF16),
        grid_spec=pltpu.PrefetchScalarGridSpec(
            num_scalar_prefetch=1, grid=(b, n_steps),
            in_specs=[tok, tok, tok, const(slopes8.shape), const(lv.shape), const(nw.shape)]
            + _page_specs(layer, n_pages, kblock, False)
            + _page_specs(layer, n_pages, vblock, False),
            out_specs=tok,
            scratch_shapes=[pltpu.VMEM(kblock, F32), pltpu.VMEM((n_maps, 1), F32),
                            pltpu.VMEM((n_maps, 1), F32), pltpu.VMEM((n_maps, DV_DF_W), F32)]),
        compiler_params=_params(("parallel", "arbitrary")),
        name="df_decode",
    )(page_table, dq, dk_new, dv_new, slopes8, lv, nw,
      *([cache_k] * PAGES_PER_STEP), *([cache_v] * PAGES_PER_STEP))


def _cache_views(cache_sb_k, cache_sb_v, cache_df_k, cache_df_v):
    d, n, p = cache_sb_k.shape[:3]
    keys_last = lambda c: jnp.transpose(
        c.reshape(d, n, p, BRANCH_W // DH_SB, DH_SB), (0, 1, 3, 4, 2))
    return (keys_last(cache_sb_k), keys_last(cache_sb_v), keys_last(cache_df_k),
            cache_df_v.reshape(d, n, p * H_DF // 8, 8, DV_DF_W))


def _lane_row(v):
    return jnp.pad(v.astype(F32), (0, LANES - v.shape[0]))[None, :]


def _row_tile(m):
    return min(m, 256)


def kernel(x_prompt, x_sample, cache_sb_k, cache_sb_v, cache_df_k, cache_df_v, state_gdn,
           state_gdn_conv, page_table, norm_w, ffn_w_in, ffn_w_out, w_in, diff_lambda,
           diff_norm_w, gdn_conv_w, gdn_a_log, gdn_dt_bias, gdn_norm_w, w_branch, w_out,
           final_norm_w):
    depth = w_in.shape[0]
    bp, tp, _ = x_prompt.shape
    bs, ts, _ = x_sample.shape
    assert ts == 1

    ffn_in = ffn_w_in.astype(BF16)
    ffn_out = ffn_w_out.astype(BF16)
    wb = w_branch.astype(BF16)
    wo = w_out.astype(BF16)
    fw = final_norm_w[None, :]
    slopes = 2.0 ** (-8.0 * jnp.arange(1, H_DF + 1, dtype=F32) / H_DF)
    slopes8 = jnp.broadcast_to(jnp.repeat(slopes, 2)[:, None], (2 * H_DF, LANES))
    ck, cv, dkc, dvc = _cache_views(cache_sb_k, cache_sb_v, cache_df_k, cache_df_v)

    xp = x_prompt.reshape(bp * tp, D_MODEL)
    xs = x_sample.reshape(bs, D_MODEL)
    tmp, tms = _row_tile(bp * tp), _row_tile(bs)
    tmp_wide = min(bp * tp, 2 * tmp)
    p_states, s_states, p_kv = [], [], None
    for l in range(depth):
        lam_init = 0.8 - 0.6 * math.exp(-0.3 * l)
        wt = jnp.transpose(w_in[l]).astype(BF16)
        nw = norm_w[l][:, None, :]
        prm = jnp.concatenate([_lane_row(gdn_a_log[l]), _lane_row(gdn_dt_bias[l]),
                               jnp.zeros((6, LANES), F32)], axis=0)
        gnw = gdn_norm_w[l][None, :]
        dnw = diff_norm_w[l][None, :]
        last = l == depth - 1

        xp = _ffn(xp, nw[0], ffn_in[l, 0], ffn_out[l, 0], fw, final=False, tm=tmp_wide)
        sq, sk_t, sv_t, dq, dk_t, dv, xc, gz, gates, gab = _proj(
            xp, nw[1], wt, tm=tmp, stack=(l, depth, bp, p_kv))
        p_kv = (sk_t, sv_t, dk_t, dv)
        r3 = lambda a: a.reshape(bp, tp, a.shape[-1])
        o_sb = _sb_prompt(r3(sq), sk_t, sv_t, l, tq=ATT_TQ)
        o_df = _df_prompt(r3(dq), dk_t, dv.reshape(depth, bp, tp, BRANCH_W), l, slopes,
                          diff_lambda[l], dnw, tq=ATT_TQ, lam_init=lam_init)
        xc3 = r3(xc)
        o_gd, s_fin = _gdn_prompt(
            xc3, r3(gab), r3(gz), gdn_conv_w[l], prm, gnw,
            jnp.zeros((bp, 8, GD_CONV_CH), F32),
            jnp.zeros((bp, H_GD, DK_GD, DV_GD), F32))
        xp = _merge(xp, o_sb.reshape(bp * tp, BRANCH_W), o_df.reshape(bp * tp, BRANCH_W),
                    o_gd.reshape(bp * tp, BRANCH_W), gates, wb[l], wo[l], tm=tmp_wide)
        xp = _ffn(xp, nw[2], ffn_in[l, 1], ffn_out[l, 1], fw, final=last, tm=tmp_wide)
        p_states.append((s_fin, xc3[:, tp - (CONV_W - 1):, :]))

        xs = _ffn(xs, nw[0], ffn_in[l, 0], ffn_out[l, 0], fw, final=False, tm=tms)
        sq, sk, sv, dq, dk, dv, xc, gz, gates, gab = _proj(xs, nw[1], wt, tm=tms)
        r3 = lambda a: a.reshape(bs, 1, a.shape[-1])
        o_sb = _sb_decode(r3(sq), ck, cv, page_table, l)
        o_df = _df_decode(r3(dq), r3(dk), r3(dv), dkc, dvc, page_table, slopes8,
                          diff_lambda[l], dnw, l, lam_init)
        o_gd, s_new = _gdn_step(r3(xc), r3(gab), r3(gz), gdn_conv_w[l], prm, gnw,
                                state_gdn_conv[l], state_gdn[l])
        xs = _merge(xs, o_sb.reshape(bs, BRANCH_W), o_df.reshape(bs, BRANCH_W),
                    o_gd.reshape(bs, BRANCH_W), gates, wb[l], wo[l], tm=tms)
        xs = _ffn(xs, nw[2], ffn_in[l, 1], ffn_out[l, 1], fw, final=last, tm=tms)
        new_buf = jnp.concatenate([state_gdn_conv[l][:, 1:, :], r3(xc)], axis=1)
        s_states.append((
            sk.reshape(bs, 1, H_SB, DH_SB), sv.reshape(bs, 1, H_SB, DH_SB),
            dk.reshape(bs, 1, H_DF, 2, DH_DF), dv.reshape(bs, 1, H_DF, DV_DF_W),
            s_new, new_buf))

    sk_t, sv_t, dk_t, dv = p_kv
    heads_last = lambda a: jnp.transpose(a.reshape(depth, bp, H_SB, DH_SB, tp), (0, 1, 4, 2, 3))
    p_kv_out = [heads_last(sk_t), heads_last(sv_t),
                jnp.transpose(dk_t.reshape(depth, bp, H_DF, 2, DH_DF, tp), (0, 1, 5, 2, 3, 4)),
                dv.reshape(depth, bp, tp, H_DF, DV_DF_W)]
    p_out = [jnp.stack(a) for a in zip(*p_states)]
    s_out = [jnp.stack(a) for a in zip(*s_states)]
    return (xp.reshape(bp, tp, D_MODEL), xs.reshape(bs, 1, D_MODEL), *p_kv_out, *p_out, *s_out)
```

```python
import functools
import math

import jax
import jax.numpy as jnp
from jax import lax
from jax.experimental import pallas as pl
from jax.experimental.pallas import tpu as pltpu

F32 = jnp.float32
BF16 = jnp.bfloat16

D_MODEL = 1024
DH_SB = 64
H_SB = 8
DH_DF = 64
H_DF = 4
DK_GD = 128
DV_GD = 128
H_GD = 4
CONV_W = 4
GD_CONV_CH = 2 * H_GD * DK_GD + H_GD * DV_GD
GDN_CHUNK = 64
GDN_CHUNKS_PER_STEP = 4
BRANCH_W = 512
N_BRANCH = 3
D_FF = 2816
NORM_EPS = 1e-6
L2_EPS = 1e-6
NEG_BIG = -1e30

LANES = 128
FF_CHUNK = 256
GAB_W = LANES
VMEM_LIMIT = 56 * 1024 * 1024

PROJ_OUTS = (
    ("sq", 512, BF16), ("sk", 512, F32), ("sv", 512, F32),
    ("dq", 512, BF16), ("dk", 512, F32), ("dv", 512, F32),
    ("xc", GD_CONV_CH, F32), ("gz", 512, F32),
    ("gates", N_BRANCH * D_MODEL, F32), ("gab", GAB_W, F32),
)


def _dot(a, b, precision=None):
    return jnp.dot(a, b, preferred_element_type=F32, precision=precision)


def _dot_nt(a, b, precision=None):
    return lax.dot_general(a, b, (((1,), (1,)), ((), ())),
                           preferred_element_type=F32, precision=precision)


def _dot_tn(a, b, precision=None):
    return lax.dot_general(a, b, (((0,), (0,)), ((), ())),
                           preferred_element_type=F32, precision=precision)


_HI = lax.Precision.HIGHEST


def _rms(x, w):
    ms = jnp.mean(x * x, axis=-1, keepdims=True)
    return x * lax.rsqrt(ms + NORM_EPS) * w


def _sigmoid(x):
    return 1.0 / (1.0 + jnp.exp(-x))


def _silu(x):
    return x * _sigmoid(x)


def _softplus(x):
    return jnp.maximum(x, 0.0) + jnp.log(1.0 + jnp.exp(-jnp.abs(x)))


def _params(sem):
    return pltpu.CompilerParams(dimension_semantics=sem, vmem_limit_bytes=VMEM_LIMIT)


def _resident(shape):
    nd = len(shape)
    return pl.BlockSpec(shape, lambda *_: (0,) * nd, pipeline_mode=pl.Buffered(1))


def _ffn_kernel(x_ref, nw_ref, win_ref, wout_ref, fw_ref, o_ref, *, final):
    x = x_ref[...]
    h = _rms(x, nw_ref[...]).astype(BF16)
    acc = jnp.zeros(x.shape, F32)
    for c in range(D_FF // FF_CHUNK):
        lo = c * FF_CHUNK
        g = _dot(h, win_ref[:, lo:lo + FF_CHUNK])
        u = _dot(h, win_ref[:, D_FF + lo:D_FF + lo + FF_CHUNK])
        a = (_silu(g) * u).astype(BF16)
        acc = acc + _dot(a, wout_ref[lo:lo + FF_CHUNK, :])
    y = x + 0.5 * acc
    if final:
        y = _rms(y, fw_ref[...])
    o_ref[...] = y


def _ffn(x, nw, w_in, w_out, fw, *, final, tm):
    m = x.shape[0]
    assert m % tm == 0
    row = pl.BlockSpec((tm, D_MODEL), lambda i: (i, 0))
    return pl.pallas_call(
        functools.partial(_ffn_kernel, final=final),
        out_shape=jax.ShapeDtypeStruct((m, D_MODEL), F32),
        grid=(m // tm,),
        in_specs=[row, _resident((1, D_MODEL)), _resident(w_in.shape),
                  _resident(w_out.shape), _resident((1, D_MODEL))],
        out_specs=row,
        compiler_params=_params(("parallel",)),
        name="ffn",
    )(x, nw, w_in, w_out, fw)


W_IN_ROWS = {"sq": (0, 512), "sk": (512, 1024), "sv": (1024, 1536), "dq": (1536, 2048),
             "dk": (2048, 2560), "dv": (2560, 3072), "xc": (3072, 4608), "gab": (4608, 4616),
             "gz": (4616, 5128), "gates": (5128, 8200)}
KV_OUTS = ("sk", "sv", "dk", "dv")
FEATURE_MAJOR = ("sk", "sv", "dk")


def _proj_kernel(x_ref, nw_ref, wt_ref, *refs, feature_major, earlier):
    o_refs = refs[len(refs) - len(PROJ_OUTS):]
    kv_in = refs[:len(refs) - len(PROJ_OUTS)]
    h = _rms(x_ref[...], nw_ref[...]).astype(BF16)
    for (name, width, dt), o_ref in zip(PROJ_OUTS, o_refs):
        r0, r1 = W_IN_ROWS[name]
        if name in KV_OUTS and feature_major:
            new = (_dot_nt(wt_ref[r0:r1, :], h) if name in feature_major
                   else _dot_nt(h, wt_ref[r0:r1, :])).astype(dt)
            for j in range(earlier):
                o_ref[j] = kv_in[j * len(KV_OUTS) + KV_OUTS.index(name)][...]
            o_ref[earlier] = new
        elif name == "gab":
            g = _dot_nt(h, wt_ref[r0:r1, :])
            o_ref[...] = jnp.concatenate(
                [g, jnp.zeros((g.shape[0], width - g.shape[1]), F32)], axis=1)
        else:
            for lo in range(0, width, 512):
                o_ref[:, lo:lo + 512] = _dot_nt(h, wt_ref[r0 + lo:r0 + lo + 512, :]).astype(dt)


def _proj(x, nw, wt, *, tm, stack=None):
    m = x.shape[0]
    assert m % tm == 0
    batch, earlier = stack if stack is not None else (1, [])
    steps = m // batch // tm
    slots = len(earlier) + 1

    def kv_spec(name, wd, n):
        if name in FEATURE_MAJOR:
            return pl.BlockSpec((n, None, wd, tm), lambda i: (0, i // steps, 0, i % steps))
        return pl.BlockSpec((n, tm, wd), lambda i: (0, i, 0))

    shapes, specs = [], []
    for name, wd, dt in PROJ_OUTS:
        if stack is None or name not in KV_OUTS:
            shapes.append(jax.ShapeDtypeStruct((m, wd), dt))
            specs.append(pl.BlockSpec((tm, wd), lambda i: (i, 0)))
        else:
            shape = (batch, wd, m // batch) if name in FEATURE_MAJOR else (m, wd)
            shapes.append(jax.ShapeDtypeStruct((slots,) + shape, dt))
            specs.append(kv_spec(name, wd, slots))
    widths = {name: wd for name, wd, _ in PROJ_OUTS}
    return pl.pallas_call(
        functools.partial(_proj_kernel, earlier=len(earlier),
                          feature_major=FEATURE_MAJOR if stack is not None else ()),
        out_shape=shapes,
        grid=(m // tm,),
        in_specs=[pl.BlockSpec((tm, D_MODEL), lambda i: (i, 0)),
                  _resident((1, D_MODEL)), _resident(wt.shape)]
        + [kv_spec(name, widths[name], None) for _ in earlier for name in KV_OUTS],
        out_specs=specs,
        compiler_params=_params(("parallel",)),
        name="proj",
    )(x, nw, wt, *[a for layer_kv in earlier for a in layer_kv])


def _merge_kernel(x_ref, osb_ref, odf_ref, ogd_ref, gates_ref, wb_ref, wo_ref, o_ref):
    m = None
    for i, b_ref in enumerate((osb_ref, odf_ref, ogd_ref)):
        p = _dot(b_ref[...], wb_ref[i])
        t = _sigmoid(gates_ref[:, i * D_MODEL:(i + 1) * D_MODEL]) * p
        m = t if m is None else m + t
    o_ref[...] = x_ref[...] + _dot(m.astype(BF16), wo_ref[...])


def _merge(x, o_sb, o_df, o_gd, gates, wb, wo, *, tm):
    m = x.shape[0]
    assert m % tm == 0
    row = lambda w: pl.BlockSpec((tm, w), lambda i: (i, 0))
    return pl.pallas_call(
        _merge_kernel,
        out_shape=jax.ShapeDtypeStruct((m, D_MODEL), F32),
        grid=(m // tm,),
        in_specs=[row(D_MODEL), row(BRANCH_W), row(BRANCH_W), row(BRANCH_W),
                  row(N_BRANCH * D_MODEL), _resident(wb.shape), _resident(wo.shape)],
        out_specs=row(D_MODEL),
        compiler_params=_params(("parallel",)),
        name="merge",
    )(x, o_sb, o_df, o_gd, gates, wb, wo)


def _log_sigmoid_pair(z):
    ls = jnp.minimum(z, 0.0) - jnp.log(1.0 + jnp.exp(-jnp.abs(z)))
    return ls, ls - z


def _suffix_sums(x, upper):
    hi = x.astype(BF16)
    lo = (x - hi.astype(F32)).astype(BF16)
    return _dot(hi, upper) + _dot(lo, upper)


ATT_TK = 256
ATT_TQ = 512
DF_BLOCKS_PER_TRIP = 4
SB_DONE = -110.0


def _stack_halves(q, seg):
    lane = lax.broadcasted_iota(jnp.int32, q.shape, 1)
    return jnp.concatenate(
        [jnp.where(lane // seg == j, q, 0.0) for j in range(LANES // seg)], axis=0).astype(BF16)


def _walk_key_blocks(qi, tq, blocks, state, per_trip):
    nd = tq // ATT_TK
    state = blocks([(qi + 1) * nd - 1 - d for d in range(nd)], state, True)
    n_free = qi * nd

    def full_trip(i, st):
        kj = n_free - 1 - per_trip * i
        return blocks([kj - d for d in range(per_trip)], st, False)

    def last_one(i, st):
        return blocks([n_free % per_trip - 1 - i], st, False)

    state = lax.fori_loop(0, n_free // per_trip, full_trip, state)
    return lax.fori_loop(0, n_free % per_trip, last_one, state)


def _sb_prompt_kernel(q_ref, k_ref, v_ref, o_ref, kb_ref, vb_ref, *, tq):
    qi = pl.program_id(2)
    tk = ATT_TK

    @pl.when(qi == 0)
    def _():
        for j in range(kb_ref.shape[0]):
            kb_ref[j] = k_ref[:, j * tk:(j + 1) * tk].astype(BF16)
            vb_ref[j] = v_ref[:, j * tk:(j + 1) * tk].astype(BF16)

    qs = _stack_halves(q_ref[...].astype(F32) * (DH_SB ** -0.5), DH_SB)
    rows = qs.shape[0]
    rowq = lax.broadcasted_iota(jnp.int32, (rows, tk), 0) % tq
    col = lax.broadcasted_iota(jnp.int32, (rows, tk), 1)
    ur = lax.broadcasted_iota(jnp.int32, (tk, tk), 0)
    uc = lax.broadcasted_iota(jnp.int32, (tk, tk), 1)
    upper = jnp.where(ur > uc, 1.0, 0.0).astype(BF16)

    def blocks(kjs, state, masked):
        carry, acc = state
        pairs = [_log_sigmoid_pair(_dot(qs, kb_ref[kj])) for kj in kjs]
        lss = [p[0] for p in pairs]
        l1ms = [p[1] for p in pairs]
        if masked:
            befores = [col + (kj * tk - qi * tq) < rowq for kj in kjs]
            l1ms = [jnp.where(b, x, 0.0) for b, x in zip(befores, l1ms)]
        sufs = [_suffix_sums(x, upper) for x in l1ms]
        carries = [carry]
        for x in l1ms:
            carries.append(carries[-1] + jnp.sum(x, axis=-1, keepdims=True))
        probs = [jnp.exp(ls + suf + c) for ls, suf, c in zip(lss, sufs, carries)]
        if masked:
            probs = [jnp.where(b, a, 0.0) for b, a in zip(befores, probs)]
        for a, kj in zip(probs, kjs):
            acc = acc + _dot_nt(a.astype(BF16), vb_ref[kj])
        return carries[-1], acc

    nd = tq // tk
    state = blocks([(qi + 1) * nd - 1 - d for d in range(nd)],
                   (jnp.zeros((rows, 1), F32), jnp.zeros((rows, LANES), F32)), True)

    def unfinished(loop):
        kj, carry, _ = loop
        return jnp.logical_and(kj >= 0, jnp.max(carry) > SB_DONE)

    def one_block(loop):
        kj, carry, acc = loop
        carry, acc = blocks([kj], (carry, acc), False)
        return kj - 1, carry, acc

    _, _, acc = lax.while_loop(unfinished, one_block, (qi * nd - 1,) + state)
    lane = lax.broadcasted_iota(jnp.int32, (tq, LANES), 1)
    o_ref[...] = jnp.where(lane < DH_SB, acc[:tq], acc[tq:]).astype(BF16)


def _sb_prompt(sq, sk_t, sv_t, layer, *, tq):
    b, t, _ = sq.shape
    assert tq % ATT_TK == 0 and t % tq == 0
    qspec = pl.BlockSpec((None, tq, LANES), lambda bi, hp, qi: (bi, qi, hp))
    kspec = pl.BlockSpec((None, None, LANES, t), lambda bi, hp, qi: (layer, bi, hp, 0))
    return pl.pallas_call(
        functools.partial(_sb_prompt_kernel, tq=tq),
        out_shape=jax.ShapeDtypeStruct((b, t, BRANCH_W), BF16),
        grid=(b, BRANCH_W // LANES, t // tq),
        in_specs=[qspec, kspec, kspec],
        out_specs=qspec,
        scratch_shapes=[pltpu.VMEM((t // ATT_TK, LANES, ATT_TK), BF16)] * 2,
        compiler_params=_params(("parallel", "parallel", "arbitrary")),
        name="sb_prompt",
    )(sq, sk_t, sv_t)


def _diff_lambda(lv, lam_init):
    a = jnp.sum(lv[0:1] * lv[1:2], axis=-1, keepdims=True)
    b = jnp.sum(lv[2:3] * lv[3:4], axis=-1, keepdims=True)
    return jnp.exp(a) - jnp.exp(b) + lam_init


def _df_prompt_kernel(slope_ref, q_ref, k_ref, v_ref, lv_ref, nw_ref, o_ref, kb_ref, vb_ref,
                      *, tq, lam_init):
    h = pl.program_id(1)
    qi = pl.program_id(2)

    tk = ATT_TK
    slope = slope_ref[h]

    @pl.when(qi == 0)
    def _():
        vb_ref[...] = v_ref[...].astype(BF16)
        lane = lax.broadcasted_iota(jnp.int32, (LANES, tk), 1)
        feat = lax.broadcasted_iota(jnp.int32, (LANES, tk), 0)
        for j in range(kb_ref.shape[0]):
            kb_ref[j, 0:LANES, :] = k_ref[:, j * tk:(j + 1) * tk].astype(BF16)
            hi, lo = _two_terms((lane + j * tk).astype(F32) * slope)
            kb_ref[j, LANES:2 * LANES, :] = jnp.where(
                feat == 0, hi, jnp.where(feat == 1, lo, jnp.where(feat == 2, 1.0, 0.0).astype(BF16)))

    qs = _stack_halves(q_ref[...].astype(F32) * (DH_DF ** -0.5), DH_DF)
    rows = qs.shape[0]
    qlane = lax.broadcasted_iota(jnp.int32, (rows, LANES), 1)
    q_bias = jnp.where(qlane < 2, 1.0, jnp.where(qlane == 2, -slope * (qi * tq).astype(F32), 0.0))
    qs = jnp.concatenate([qs, q_bias.astype(BF16)], axis=1)
    rowq = lax.broadcasted_iota(jnp.int32, (rows, tk), 0) % tq
    col = lax.broadcasted_iota(jnp.int32, (rows, tk), 1)

    def block(kj, state, masked):
        m, l, acc = state
        start = pl.multiple_of(kj * tk, tk)
        offset = qi * tq - kj * tk
        s = _dot(qs, kb_ref[kj])
        if masked:
            visible = col - offset <= rowq
            s = jnp.where(visible, s, NEG_BIG)
        m_new = jnp.maximum(m, jnp.max(s, axis=-1, keepdims=True))
        p = jnp.exp(s - m_new)
        if masked:
            p = jnp.where(visible, p, 0.0)
        alpha = jnp.exp(m - m_new)
        l = alpha * l + jnp.sum(p, axis=-1, keepdims=True)
        acc = alpha * acc + _dot(p.astype(BF16), vb_ref[pl.ds(start, tk), :])
        return m_new, l, acc

    def blocks(kjs, state, masked):
        for kj in kjs:
            state = block(kj, state, masked)
        return state

    _, l, acc = _walk_key_blocks(
        qi, tq, blocks, (jnp.full((rows, 1), NEG_BIG, F32), jnp.zeros((rows, 1), F32),
                         jnp.zeros((rows, LANES), F32)), DF_BLOCKS_PER_TRIP)
    lam = _diff_lambda(lv_ref[...], lam_init)
    o = acc[:tq] / l[:tq] - lam * (acc[tq:] / l[tq:])
    o_ref[...] = (_rms(o, nw_ref[...]) * (1.0 - lam_init)).astype(BF16)


def _df_prompt(dq, dk_t, dv, layer, slopes, lv, nw, *, tq, lam_init):
    b, t, _ = dq.shape
    assert tq % ATT_TK == 0 and t % tq == 0
    qspec = pl.BlockSpec((None, tq, LANES), lambda bi, h, qi: (bi, qi, h))
    kspec = pl.BlockSpec((None, None, LANES, t), lambda bi, h, qi: (layer, bi, h, 0))
    vspec = pl.BlockSpec((None, None, t, LANES), lambda bi, h, qi: (layer, bi, 0, h))
    return pl.pallas_call(
        functools.partial(_df_prompt_kernel, tq=tq, lam_init=lam_init),
        out_shape=jax.ShapeDtypeStruct((b, t, BRANCH_W), BF16),
        grid=(b, H_DF, t // tq),
        in_specs=[pl.BlockSpec(memory_space=pltpu.SMEM), qspec, kspec, vspec,
                  _resident(lv.shape), _resident(nw.shape)],
        out_specs=qspec,
        scratch_shapes=[pltpu.VMEM((t // ATT_TK, 2 * LANES, ATT_TK), BF16),
                        pltpu.VMEM((t, LANES), BF16)],
        compiler_params=_params(("parallel", "parallel", "arbitrary")),
        name="df_prompt",
    )(slopes, dq, dk_t, dv, lv, nw)


def _gdn_gates(gab, prm):
    g = -jnp.exp(prm[0:1, :]) * _softplus(gab + prm[1:2, :])
    return g, _sigmoid(gab)


def _l2n(x):
    return x * lax.rsqrt(jnp.sum(x * x, axis=-1, keepdims=True) + L2_EPS)


def _two_terms(x):
    hi = x.astype(BF16)
    return hi, (x - hi.astype(F32)).astype(BF16)


def _split_lhs(a):
    hi, lo = _two_terms(a)
    return jnp.concatenate([hi, hi, lo], axis=1)


def _split_rhs(b):
    hi, lo = _two_terms(b)
    return jnp.concatenate([hi, lo, hi], axis=0)


def _gdn_out(o, nw, z):
    return (_rms(o, nw) * _silu(z)).astype(BF16)


def _gdn_prompt_kernel(xc_ref, gab_ref, gz_ref, cw_ref, prm_ref, nw_ref, cb_ref, s0_ref,
                       o_ref, sfin_ref, buf_ref, st_ref, *, n_steps):
    c = pl.program_id(1)
    ch = GDN_CHUNK
    rows = GDN_CHUNKS_PER_STEP * ch

    @pl.when(c == 0)
    def _():
        buf_ref[0:8, :] = cb_ref[...]
        st_ref[...] = s0_ref[...]

    buf_ref[8:8 + rows, :] = xc_ref[...]
    xconv = None
    for j in range(CONV_W):
        t = cw_ref[j:j + 1, :] * buf_ref[5 + j:5 + j + rows, :]
        xconv = t if xconv is None else xconv + t
    buf_ref[0:8, :] = buf_ref[rows:rows + 8, :]
    xs = _silu(xconv)

    g_all, beta_all = _gdn_gates(gab_ref[...], prm_ref[...])
    row = lax.broadcasted_iota(jnp.int32, (ch, ch), 0)
    col = lax.broadcasted_iota(jnp.int32, (ch, ch), 1)
    tri = col <= row
    strict = col < row
    eye = jnp.where(row == col, 1.0, 0.0)
    lower = jnp.where(tri, 1.0, 0.0)
    hk = H_GD * DK_GD

    problems = [(j, h) for j in range(GDN_CHUNKS_PER_STEP) for h in range(H_GD)]
    cums = []
    for j in range(GDN_CHUNKS_PER_STEP):
        cum_all = _dot(lower, g_all[j * ch:(j + 1) * ch], _HI)
        cums.append((cum_all, cum_all.T))

    def first_stage(j, h):
        r0 = j * ch
        q = _l2n(xs[r0:r0 + ch, h * DK_GD:(h + 1) * DK_GD]) * (DK_GD ** -0.5)
        k = _l2n(xs[r0:r0 + ch, hk + h * DK_GD:hk + (h + 1) * DK_GD])
        v = xs[r0:r0 + ch, 2 * hk + h * DV_GD:2 * hk + (h + 1) * DV_GD]
        gc = cums[j][0][:, h:h + 1]
        gr = cums[j][1][h:h + 1, :]
        beta = beta_all[r0:r0 + ch, H_GD + h:H_GD + h + 1]
        decay = jnp.exp(jnp.where(tri, gc - gr, NEG_BIG))
        eg = jnp.exp(gc)
        g_last = gc[ch - 1:ch, :]
        kb = k.astype(BF16)
        qk = (_dot_nt(q.astype(BF16), kb) * decay).astype(BF16)
        pw = -jnp.where(strict, beta * decay * _dot_nt(kb, kb), 0.0)
        rhs = _split_rhs(jnp.concatenate([beta * v, (beta * eg) * k], axis=1))
        carry_over = ((q * eg).astype(BF16), qk, (k * jnp.exp(g_last - gc)).astype(BF16),
                      jnp.exp(g_last))
        return pw, rhs, carry_over

    staged = [first_stage(j, h) for j, h in problems]
    pws = [p[0] for p in staged]
    invs = [eye + pw for pw in pws]
    for _ in range(5):
        pws = [_dot(_split_lhs(pw), _split_rhs(pw)) for pw in pws]
        invs = [inv + _dot(_split_lhs(inv), _split_rhs(pw)) for inv, pw in zip(invs, pws)]
    sols = [_dot(_split_lhs(inv), p[1]) for inv, p in zip(invs, staged)]
    prepared = {jh: (sol[:, :DV_GD], sol[:, DV_GD:].astype(BF16)) + p[2]
                for jh, sol, p in zip(problems, sols, staged)}

    states = [st_ref[h] for h in range(H_GD)]
    for j in range(GDN_CHUNKS_PER_STEP):
        for h in range(H_GD):
            sol_v, sol_k, q_eg, qk, k_dec, e_last = prepared[(j, h)]
            s = states[h]
            sb = s.astype(BF16)
            ub = (sol_v - _dot(sol_k, sb)).astype(BF16)
            o = _dot(q_eg, sb) + _dot(qk, ub)
            states[h] = e_last * s + _dot_tn(k_dec, ub)
            o_ref[j * ch:(j + 1) * ch, h * DV_GD:(h + 1) * DV_GD] = _gdn_out(
                o, nw_ref[...], gz_ref[j * ch:(j + 1) * ch, h * DV_GD:(h + 1) * DV_GD])
    for h in range(H_GD):
        st_ref[h] = states[h]

    @pl.when(c == n_steps - 1)
    def _():
        sfin_ref[...] = st_ref[...]


def _gdn_prompt(xc, gab, gz, cw, prm, nw, cb8, s0):
    b, t, _ = xc.shape
    rows = GDN_CHUNKS_PER_STEP * GDN_CHUNK
    assert t % rows == 0
    n_steps = t // rows
    tok = lambda w: pl.BlockSpec((None, rows, w), lambda bi, c: (bi, c, 0))
    st = pl.BlockSpec((None, H_GD, DK_GD, DV_GD), lambda bi, c: (bi, 0, 0, 0))
    return pl.pallas_call(
        functools.partial(_gdn_prompt_kernel, n_steps=n_steps),
        out_shape=[jax.ShapeDtypeStruct((b, t, BRANCH_W), BF16),
                   jax.ShapeDtypeStruct((b, H_GD, DK_GD, DV_GD), F32)],
        grid=(b, n_steps),
        in_specs=[tok(GD_CONV_CH), tok(GAB_W), tok(BRANCH_W), _resident(cw.shape),
                  _resident(prm.shape), _resident(nw.shape),
                  pl.BlockSpec((None, 8, GD_CONV_CH), lambda bi, c: (bi, 0, 0)), st],
        out_specs=[tok(BRANCH_W), st],
        scratch_shapes=[pltpu.VMEM((rows + 8, GD_CONV_CH), F32),
                        pltpu.VMEM((H_GD, DK_GD, DV_GD), F32)],
        compiler_params=_params(("parallel", "arbitrary")),
        name="gdn_prompt",
    )(xc, gab, gz, cw, prm, nw, cb8, s0)


def _gdn_step_kernel(xc_ref, gab_ref, gz_ref, cw_ref, prm_ref, nw_ref, cb_ref, s0_ref,
                     o_ref, sn_ref):
    xconv = cw_ref[CONV_W - 1:CONV_W, :] * xc_ref[...]
    for j in range(CONV_W - 1):
        xconv = xconv + cw_ref[j:j + 1, :] * cb_ref[j:j + 1, :]
    xs = _silu(xconv)
    g_all, beta_all = _gdn_gates(gab_ref[...], prm_ref[...])
    first = lax.broadcasted_iota(jnp.int32, (8, DK_GD), 0) == 0
    rows8 = lambda x: jnp.where(first, jnp.broadcast_to(x, (8, x.shape[-1])), 0.0)
    hk = H_GD * DK_GD
    for h in range(H_GD):
        q = rows8(_l2n(xs[:, h * DK_GD:(h + 1) * DK_GD]) * (DK_GD ** -0.5))
        k = rows8(_l2n(xs[:, hk + h * DK_GD:hk + (h + 1) * DK_GD]))
        v = xs[:, 2 * hk + h * DV_GD:2 * hk + (h + 1) * DV_GD]
        eg = jnp.exp(g_all[:, h:h + 1])
        beta = beta_all[:, H_GD + h:H_GD + h + 1]
        s = s0_ref[h]
        u = beta * (v - eg * _dot(k, s, _HI))
        s_new = eg * s + _dot_tn(k, u, _HI)
        sn_ref[h] = s_new
        o = _dot(q, s_new, _HI)[0:1, :]
        o_ref[:, h * DV_GD:(h + 1) * DV_GD] = _gdn_out(
            o, nw_ref[...], gz_ref[:, h * DV_GD:(h + 1) * DV_GD])


def _gdn_step(xc, gab, gz, cw, prm, nw, cb, s0):
    b = xc.shape[0]
    tok = lambda w: pl.BlockSpec((None, 1, w), lambda bi: (bi, 0, 0))
    st = pl.BlockSpec((None, H_GD, DK_GD, DV_GD), lambda bi: (bi, 0, 0, 0))
    return pl.pallas_call(
        _gdn_step_kernel,
        out_shape=[jax.ShapeDtypeStruct((b, 1, BRANCH_W), BF16),
                   jax.ShapeDtypeStruct((b, H_GD, DK_GD, DV_GD), F32)],
        grid=(b,),
        in_specs=[tok(GD_CONV_CH), tok(GAB_W), tok(BRANCH_W), _resident(cw.shape),
                  _resident(prm.shape), _resident(nw.shape),
                  pl.BlockSpec((None, CONV_W - 1, GD_CONV_CH), lambda bi: (bi, 0, 0)), st],
        out_specs=[tok(BRANCH_W), st],
        compiler_params=_params(("parallel",)),
        name="gdn_step",
    )(xc, gab, gz, cw, prm, nw, cb, s0)


PAGES_PER_STEP = 16


def _page_specs(layer, n_pages, block, descending):
    specs = []
    for i in range(PAGES_PER_STEP):
        def index(b, g, pt, i=i):
            p = g * PAGES_PER_STEP + i
            if descending:
                p = n_pages - 1 - p
            return (layer, pt[b, p]) + (0,) * len(block)
        specs.append(pl.BlockSpec((None, None) + block, index))
    return specs


def _store_lane_columns(q, qc_ref):
    n_seg, seg, lanes = qc_ref.shape
    first = lax.broadcasted_iota(jnp.int32, (8, seg), 0) == 0
    ones = jnp.where(lax.broadcasted_iota(jnp.int32, (8, lanes), 0) == 0, 1.0, 0.0).astype(BF16)
    for s in range(n_seg):
        qs = jnp.where(first, jnp.broadcast_to(q[:, s * seg:(s + 1) * seg], (8, seg)), 0.0)
        qc_ref[s] = _dot_tn(qs.astype(BF16), ones)


def _page_scores(k_ref, qc_ref):
    return jnp.concatenate(
        [jnp.sum(k_ref[s] * qc_ref[s], axis=0, keepdims=True) for s in range(k_ref.shape[0])],
        axis=0)


def _head_rows(q, n_rows, seg):
    w = q.shape[-1]
    r = lax.broadcasted_iota(jnp.int32, (n_rows, w), 0)
    c = lax.broadcasted_iota(jnp.int32, (n_rows, w), 1)
    return jnp.where(c // seg == r, jnp.broadcast_to(q, (n_rows, w)), 0.0)


SB_PAGES_PER_GROUP = 2


def _sb_decode_kernel(pt_ref, q_ref, k_hbm, v_hbm, o_ref, kbuf, vbuf, sems, qc_ref, acc_ref,
                      *, layer, n_pages):
    b = pl.program_id(0)
    grp = SB_PAGES_PER_GROUP
    n_groups = n_pages // grp
    page_rows = qc_ref.shape[-1]

    def copies(g):
        slot = g % 2
        out = []
        for i in range(grp):
            page = pt_ref[b, n_pages - 1 - (g * grp + i)]
            out.append(pltpu.make_async_copy(k_hbm.at[layer, page], kbuf.at[slot, i], sems.at[slot, 0, i]))
            out.append(pltpu.make_async_copy(v_hbm.at[layer, page], vbuf.at[slot, i], sems.at[slot, 1, i]))
        return out

    for c in copies(0):
        c.start()
    _store_lane_columns(q_ref[...].astype(F32) * (DH_SB ** -0.5), qc_ref)
    acc_ref[...] = jnp.zeros(acc_ref.shape, F32)
    row = lax.broadcasted_iota(jnp.int32, (page_rows, page_rows), 0)
    col = lax.broadcasted_iota(jnp.int32, (page_rows, page_rows), 1)
    upper = jnp.where(row > col, 1.0, 0.0).astype(BF16)

    def unfinished(loop):
        g, carry = loop
        return jnp.logical_and(g < n_groups, jnp.max(carry) > SB_DONE)

    def one_group(loop):
        g, carry = loop
        slot = g % 2

        @pl.when(g + 1 < n_groups)
        def _():
            for c in copies(g + 1):
                c.start()

        for c in copies(g):
            c.wait()
        z = jnp.concatenate([_page_scores(kbuf.at[slot, i], qc_ref) for i in range(grp)], axis=0)
        ls, l1m = _log_sigmoid_pair(z)
        page_sums = jnp.sum(l1m, axis=-1, keepdims=True)
        carries = [carry]
        for i in range(grp):
            carries.append(carries[-1] + page_sums[i * H_SB:(i + 1) * H_SB])
        a = jnp.exp(ls + _suffix_sums(l1m, upper) + jnp.concatenate(carries[:-1], axis=0))
        for h in range(H_SB):
            t = acc_ref[h]
            for i in range(grp):
                t = t + vbuf[slot, i, h] * a[i * H_SB + h:i * H_SB + h + 1, :]
            acc_ref[h] = t
        return g + 1, carries[-1]

    g_end, _ = lax.while_loop(unfinished, one_group, (0, jnp.zeros((H_SB, 1), F32)))

    @pl.when(g_end < n_groups)
    def _():
        for c in copies(g_end):
            c.wait()

    ones = jnp.ones((8, page_rows), F32)
    o_ref[...] = jnp.concatenate(
        [_dot_nt(ones, acc_ref[h], _HI)[0:1, :] for h in range(H_SB)], axis=1).astype(BF16)


def _sb_decode(sq, cache_k, cache_v, page_table, layer):
    b = sq.shape[0]
    n_pages = page_table.shape[1]
    block = cache_k.shape[2:]
    grp = SB_PAGES_PER_GROUP
    assert n_pages % grp == 0
    tok = pl.BlockSpec((None, 1, BRANCH_W), lambda bi, pt: (bi, 0, 0))
    hbm = pl.BlockSpec(memory_space=pl.ANY)
    return pl.pallas_call(
        functools.partial(_sb_decode_kernel, layer=layer, n_pages=n_pages),
        out_shape=jax.ShapeDtypeStruct((b, 1, BRANCH_W), BF16),
        grid_spec=pltpu.PrefetchScalarGridSpec(
            num_scalar_prefetch=1, grid=(b,),
            in_specs=[tok, hbm, hbm], out_specs=tok,
            scratch_shapes=[pltpu.VMEM((2, grp) + block, F32), pltpu.VMEM((2, grp) + block, F32),
                            pltpu.SemaphoreType.DMA((2, 2, grp)),
                            pltpu.VMEM(block, F32), pltpu.VMEM(block, F32)]),
        compiler_params=_params(("arbitrary",)),
        name="sb_decode",
    )(page_table, sq, cache_k, cache_v)


def _df_decode_kernel(pt_ref, q_ref, kn_ref, vn_ref, slope_ref, lv_ref, nw_ref, *refs,
                      n_steps, lam_init):
    k_refs = refs[:PAGES_PER_STEP]
    v_refs = refs[PAGES_PER_STEP:2 * PAGES_PER_STEP]
    o_ref, qc_ref, m_ref, l_ref, acc_ref = refs[2 * PAGES_PER_STEP:]
    g = pl.program_id(1)
    n_maps = 2 * H_DF
    page_rows = qc_ref.shape[-1]
    past = n_steps * PAGES_PER_STEP * page_rows
    q = q_ref[...].astype(F32) * (DH_DF ** -0.5)

    @pl.when(g == 0)
    def _():
        _store_lane_columns(q, qc_ref)
        m_ref[...] = jnp.full(m_ref.shape, NEG_BIG, F32)
        l_ref[...] = jnp.zeros(l_ref.shape, F32)
        acc_ref[...] = jnp.zeros(acc_ref.shape, F32)

    rows = PAGES_PER_STEP * n_maps
    slope = jnp.concatenate([slope_ref[...][:, 0:1]] * PAGES_PER_STEP, axis=0)
    page_of_row = lax.broadcasted_iota(jnp.int32, (rows, page_rows), 0) // n_maps
    within = lax.broadcasted_iota(jnp.int32, (rows, page_rows), 1)
    first_key = (g * PAGES_PER_STEP + page_of_row) * page_rows
    dist = (past - first_key - within).astype(F32)
    flat = H_DF * page_rows
    ek = lax.broadcasted_iota(jnp.int32, (page_rows, flat), 0)
    er = lax.broadcasted_iota(jnp.int32, (page_rows, flat), 1)
    expand = jnp.where(er // H_DF == ek, 1.0, 0.0).astype(BF16)
    mr = lax.broadcasted_iota(jnp.int32, (rows, flat), 0)
    mc = lax.broadcasted_iota(jnp.int32, (rows, flat), 1)
    own_head = mc % H_DF == (mr % n_maps) // 2
    m, l, acc = m_ref[...], l_ref[...], acc_ref[...]

    def over_pages(op, x):
        return functools.reduce(op, [x[i * n_maps:(i + 1) * n_maps] for i in range(PAGES_PER_STEP)])

    s = jnp.concatenate([_page_scores(k_ref, qc_ref) for k_ref in k_refs], axis=0) - slope * dist
    m_new = jnp.maximum(m, over_pages(jnp.maximum, jnp.max(s, axis=-1, keepdims=True)))
    p = jnp.exp(s - jnp.concatenate([m_new] * PAGES_PER_STEP, axis=0))
    alpha = jnp.exp(m - m_new)
    pe = jnp.where(own_head, _dot(p.astype(BF16), expand), 0.0)
    pe_all = jnp.concatenate(
        [pe[i * n_maps:(i + 1) * n_maps] for i in range(PAGES_PER_STEP)], axis=1).astype(BF16)
    v_all = jnp.concatenate(
        [v_ref[...].reshape(flat, DV_DF_W).astype(BF16) for v_ref in v_refs], axis=0)
    acc = alpha * acc + _dot(pe_all, v_all)
    m_ref[...] = m_new
    l_ref[...] = alpha * l + over_pages(jnp.add, jnp.sum(p, axis=-1, keepdims=True))
    acc_ref[...] = acc

    @pl.when(g == n_steps - 1)
    def _():
        s_own = jnp.sum(_head_rows(q, n_maps, DH_DF) * kn_ref[...], axis=-1, keepdims=True)
        v_own = jnp.concatenate(
            [vn_ref[:, (r // 2) * DV_DF_W:(r // 2 + 1) * DV_DF_W] for r in range(n_maps)], axis=0)
        m_last, l_last, acc_last = m_ref[...], l_ref[...], acc_ref[...]
        mf = jnp.maximum(m_last, s_own)
        p_own = jnp.exp(s_own - mf)
        scale_last = jnp.exp(m_last - mf)
        lf = scale_last * l_last + p_own
        accf = scale_last * acc_last + p_own * v_own
        lam = _diff_lambda(lv_ref[...], lam_init)
        o_map = accf / lf
        for h in range(H_DF):
            o = o_map[2 * h:2 * h + 1, :] - lam * o_map[2 * h + 1:2 * h + 2, :]
            o_ref[:, h * DV_DF_W:(h + 1) * DV_DF_W] = (
                _rms(o, nw_ref[...]) * (1.0 - lam_init)).astype(BF16)


DV_DF_W = 2 * DH_DF


def _df_decode(dq, dk_new, dv_new, cache_k, cache_v, page_table, slopes8, lv, nw, layer,
               lam_init):
    b = dq.shape[0]
    n_pages = page_table.shape[1]
    kblock, vblock = cache_k.shape[2:], cache_v.shape[2:]
    assert n_pages % PAGES_PER_STEP == 0
    n_steps = n_pages // PAGES_PER_STEP
    tok = pl.BlockSpec((None, 1, BRANCH_W), lambda bi, g, pt: (bi, 0, 0))
    const = lambda shape: pl.BlockSpec(shape, lambda bi, g, pt: (0,) * len(shape))
    n_maps = 2 * H_DF
    return pl.pallas_call(
        functools.partial(_df_decode_kernel, n_steps=n_steps, lam_init=lam_init),
        out_shape=jax.ShapeDtypeStruct((b, 1, BRANCH_W), BF16),
        grid_spec=pltpu.PrefetchScalarGridSpec(
            num_scalar_prefetch=1, grid=(b, n_steps),
            in_specs=[tok, tok, tok, const(slopes8.shape), const(lv.shape), const(nw.shape)]
            + _page_specs(layer, n_pages, kblock, False)
            + _page_specs(layer, n_pages, vblock, False),
            out_specs=tok,
            scratch_shapes=[pltpu.VMEM(kblock, F32), pltpu.VMEM((n_maps, 1), F32),
                            pltpu.VMEM((n_maps, 1), F32), pltpu.VMEM((n_maps, DV_DF_W), F32)]),
        compiler_params=_params(("parallel", "arbitrary")),
        name="df_decode",
    )(page_table, dq, dk_new, dv_new, slopes8, lv, nw,
      *([cache_k] * PAGES_PER_STEP), *([cache_v] * PAGES_PER_STEP))


def _cache_views(cache_sb_k, cache_sb_v, cache_df_k, cache_df_v):
    d, n, p = cache_sb_k.shape[:3]
    keys_last = lambda c: jnp.transpose(
        c.reshape(d, n, p, BRANCH_W // DH_SB, DH_SB), (0, 1, 3, 4, 2))
    return (keys_last(cache_sb_k), keys_last(cache_sb_v), keys_last(cache_df_k),
            cache_df_v.reshape(d, n, p * H_DF // 8, 8, DV_DF_W))


def _lane_row(v):
    return jnp.pad(v.astype(F32), (0, LANES - v.shape[0]))[None, :]


def _row_tile(m):
    return min(m, 256)


def kernel(x_prompt, x_sample, cache_sb_k, cache_sb_v, cache_df_k, cache_df_v, state_gdn,
           state_gdn_conv, page_table, norm_w, ffn_w_in, ffn_w_out, w_in, diff_lambda,
           diff_norm_w, gdn_conv_w, gdn_a_log, gdn_dt_bias, gdn_norm_w, w_branch, w_out,
           final_norm_w):
    depth = w_in.shape[0]
    bp, tp, _ = x_prompt.shape
    bs, ts, _ = x_sample.shape
    assert ts == 1

    ffn_in = ffn_w_in.astype(BF16)
    ffn_out = ffn_w_out.astype(BF16)
    wb = w_branch.astype(BF16)
    wo = w_out.astype(BF16)
    fw = final_norm_w[None, :]
    slopes = 2.0 ** (-8.0 * jnp.arange(1, H_DF + 1, dtype=F32) / H_DF)
    slopes8 = jnp.broadcast_to(jnp.repeat(slopes, 2)[:, None], (2 * H_DF, LANES))
    ck, cv, dkc, dvc = _cache_views(cache_sb_k, cache_sb_v, cache_df_k, cache_df_v)

    xp = x_prompt.reshape(bp * tp, D_MODEL)
    xs = x_sample.reshape(bs, D_MODEL)
    tmp, tms = _row_tile(bp * tp), _row_tile(bs)
    tmp_wide = min(bp * tp, 2 * tmp)
    p_states, s_states, p_kv = [], [], []
    for l in range(depth):
        lam_init = 0.8 - 0.6 * math.exp(-0.3 * l)
        wt = jnp.transpose(w_in[l]).astype(BF16)
        nw = norm_w[l][:, None, :]
        prm = jnp.concatenate([_lane_row(gdn_a_log[l]), _lane_row(gdn_dt_bias[l]),
                               jnp.zeros((6, LANES), F32)], axis=0)
        gnw = gdn_norm_w[l][None, :]
        dnw = diff_norm_w[l][None, :]
        last = l == depth - 1

        xp = _ffn(xp, nw[0], ffn_in[l, 0], ffn_out[l, 0], fw, final=False, tm=tmp_wide)
        sq, sk_t, sv_t, dq, dk_t, dv, xc, gz, gates, gab = _proj(
            xp, nw[1], wt, tm=tmp, stack=(bp, p_kv if last else []))
        p_kv.append((sk_t, sv_t, dk_t, dv))
        slot = sk_t.shape[0] - 1
        r3 = lambda a: a.reshape(bp, tp, a.shape[-1])
        o_sb = _sb_prompt(r3(sq), sk_t, sv_t, slot, tq=ATT_TQ)
        o_df = _df_prompt(r3(dq), dk_t, dv.reshape(slot + 1, bp, tp, BRANCH_W), slot, slopes,
                          diff_lambda[l], dnw, tq=ATT_TQ, lam_init=lam_init)
        xc3 = r3(xc)
        o_gd, s_fin = _gdn_prompt(
            xc3, r3(gab), r3(gz), gdn_conv_w[l], prm, gnw,
            jnp.zeros((bp, 8, GD_CONV_CH), F32),
            jnp.zeros((bp, H_GD, DK_GD, DV_GD), F32))
        xp = _merge(xp, o_sb.reshape(bp * tp, BRANCH_W), o_df.reshape(bp * tp, BRANCH_W),
                    o_gd.reshape(bp * tp, BRANCH_W), gates, wb[l], wo[l], tm=tmp_wide)
        xp = _ffn(xp, nw[2], ffn_in[l, 1], ffn_out[l, 1], fw, final=last, tm=tmp_wide)
        p_states.append((s_fin, xc3[:, tp - (CONV_W - 1):, :]))

        xs = _ffn(xs, nw[0], ffn_in[l, 0], ffn_out[l, 0], fw, final=False, tm=tms)
        sq, sk, sv, dq, dk, dv, xc, gz, gates, gab = _proj(xs, nw[1], wt, tm=tms)
        r3 = lambda a: a.reshape(bs, 1, a.shape[-1])
        o_sb = _sb_decode(r3(sq), ck, cv, page_table, l)
        o_df = _df_decode(r3(dq), r3(dk), r3(dv), dkc, dvc, page_table, slopes8,
                          diff_lambda[l], dnw, l, lam_init)
        o_gd, s_new = _gdn_step(r3(xc), r3(gab), r3(gz), gdn_conv_w[l], prm, gnw,
                                state_gdn_conv[l], state_gdn[l])
        xs = _merge(xs, o_sb.reshape(bs, BRANCH_W), o_df.reshape(bs, BRANCH_W),
                    o_gd.reshape(bs, BRANCH_W), gates, wb[l], wo[l], tm=tms)
        xs = _ffn(xs, nw[2], ffn_in[l, 1], ffn_out[l, 1], fw, final=last, tm=tms)
        new_buf = jnp.concatenate([state_gdn_conv[l][:, 1:, :], r3(xc)], axis=1)
        s_states.append((
            sk.reshape(bs, 1, H_SB, DH_SB), sv.reshape(bs, 1, H_SB, DH_SB),
            dk.reshape(bs, 1, H_DF, 2, DH_DF), dv.reshape(bs, 1, H_DF, DV_DF_W),
            s_new, new_buf))

    sk_t, sv_t, dk_t, dv = p_kv[-1]
    heads_last = lambda a: jnp.transpose(a.reshape(depth, bp, H_SB, DH_SB, tp), (0, 1, 4, 2, 3))
    p_kv_out = [heads_last(sk_t), heads_last(sv_t),
                jnp.transpose(dk_t.reshape(depth, bp, H_DF, 2, DH_DF, tp), (0, 1, 5, 2, 3, 4)),
                dv.reshape(depth, bp, tp, H_DF, DV_DF_W)]
    p_out = [jnp.stack(a) for a in zip(*p_states)]
    s_out = [jnp.stack(a) for a in zip(*s_states)]
    return (xp.reshape(bp, tp, D_MODEL), xs.reshape(bs, 1, D_MODEL), *p_kv_out, *p_out, *s_out)
```

```python
import functools
import math

import jax
import jax.numpy as jnp
from jax import lax
from jax.experimental import pallas as pl
from jax.experimental.pallas import tpu as pltpu

F32 = jnp.float32
BF16 = jnp.bfloat16

D_MODEL = 1024
DH_SB = 64
H_SB = 8
DH_DF = 64
H_DF = 4
DK_GD = 128
DV_GD = 128
H_GD = 4
CONV_W = 4
GD_CONV_CH = 2 * H_GD * DK_GD + H_GD * DV_GD
GDN_CHUNK = 64
GDN_CHUNKS_PER_STEP = 4
BRANCH_W = 512
N_BRANCH = 3
D_FF = 2816
NORM_EPS = 1e-6
L2_EPS = 1e-6
NEG_BIG = -1e30

LANES = 128
FF_CHUNK = 256
GAB_W = LANES
VMEM_LIMIT = 56 * 1024 * 1024

PROJ_OUTS = (
    ("sq", 512, BF16), ("sk", 512, F32), ("sv", 512, F32),
    ("dq", 512, BF16), ("dk", 512, F32), ("dv", 512, F32),
    ("xc", GD_CONV_CH, F32), ("gz", 512, F32),
    ("gates", N_BRANCH * D_MODEL, F32), ("gab", GAB_W, F32),
)


def _dot(a, b, precision=None):
    return jnp.dot(a, b, preferred_element_type=F32, precision=precision)


def _dot_nt(a, b, precision=None):
    return lax.dot_general(a, b, (((1,), (1,)), ((), ())),
                           preferred_element_type=F32, precision=precision)


def _dot_tn(a, b, precision=None):
    return lax.dot_general(a, b, (((0,), (0,)), ((), ())),
                           preferred_element_type=F32, precision=precision)


_HI = lax.Precision.HIGHEST


def _rms(x, w):
    ms = jnp.mean(x * x, axis=-1, keepdims=True)
    return x * lax.rsqrt(ms + NORM_EPS) * w


def _sigmoid(x):
    return 1.0 / (1.0 + jnp.exp(-x))


def _silu(x):
    return x * _sigmoid(x)


def _softplus(x):
    return jnp.maximum(x, 0.0) + jnp.log(1.0 + jnp.exp(-jnp.abs(x)))


def _params(sem):
    return pltpu.CompilerParams(dimension_semantics=sem, vmem_limit_bytes=VMEM_LIMIT)


def _resident(shape):
    nd = len(shape)
    return pl.BlockSpec(shape, lambda *_: (0,) * nd, pipeline_mode=pl.Buffered(1))


def _ffn_kernel(x_ref, nw_ref, win_ref, wout_ref, fw_ref, o_ref, *, final):
    x = x_ref[...]
    h = _rms(x, nw_ref[...]).astype(BF16)
    acc = jnp.zeros(x.shape, F32)
    for c in range(D_FF // FF_CHUNK):
        lo = c * FF_CHUNK
        g = _dot(h, win_ref[:, lo:lo + FF_CHUNK])
        u = _dot(h, win_ref[:, D_FF + lo:D_FF + lo + FF_CHUNK])
        a = (_silu(g) * u).astype(BF16)
        acc = acc + _dot(a, wout_ref[lo:lo + FF_CHUNK, :])
    y = x + 0.5 * acc
    if final:
        y = _rms(y, fw_ref[...])
    o_ref[...] = y


def _ffn(x, nw, w_in, w_out, fw, *, final, tm):
    m = x.shape[0]
    assert m % tm == 0
    row = pl.BlockSpec((tm, D_MODEL), lambda i: (i, 0))
    return pl.pallas_call(
        functools.partial(_ffn_kernel, final=final),
        out_shape=jax.ShapeDtypeStruct((m, D_MODEL), F32),
        grid=(m // tm,),
        in_specs=[row, _resident((1, D_MODEL)), _resident(w_in.shape),
                  _resident(w_out.shape), _resident((1, D_MODEL))],
        out_specs=row,
        compiler_params=_params(("parallel",)),
        name="ffn",
    )(x, nw, w_in, w_out, fw)


W_IN_ROWS = {"sq": (0, 512), "sk": (512, 1024), "sv": (1024, 1536), "dq": (1536, 2048),
             "dk": (2048, 2560), "dv": (2560, 3072), "xc": (3072, 4608), "gab": (4608, 4616),
             "gz": (4616, 5128), "gates": (5128, 8200)}
KV_OUTS = ("sk", "sv", "dk", "dv")
FEATURE_MAJOR = ("sk", "sv", "dk")


def _proj_kernel(x_ref, nw_ref, wt_ref, *refs, feature_major, earlier):
    o_refs = refs[len(refs) - len(PROJ_OUTS):]
    kv_in = refs[:len(refs) - len(PROJ_OUTS)]
    h = _rms(x_ref[...], nw_ref[...]).astype(BF16)
    for (name, width, dt), o_ref in zip(PROJ_OUTS, o_refs):
        r0, r1 = W_IN_ROWS[name]
        if name in KV_OUTS and feature_major:
            new = (_dot_nt(wt_ref[r0:r1, :], h) if name in feature_major
                   else _dot_nt(h, wt_ref[r0:r1, :])).astype(dt)
            for j in range(earlier):
                o_ref[j] = kv_in[j * len(KV_OUTS) + KV_OUTS.index(name)][...]
            o_ref[earlier] = new
        elif name == "gab":
            g = _dot_nt(h, wt_ref[r0:r1, :])
            o_ref[...] = jnp.concatenate(
                [g, jnp.zeros((g.shape[0], width - g.shape[1]), F32)], axis=1)
        else:
            for lo in range(0, width, 512):
                o_ref[:, lo:lo + 512] = _dot_nt(h, wt_ref[r0 + lo:r0 + lo + 512, :]).astype(dt)


def _proj(x, nw, wt, *, tm, stack=None):
    m = x.shape[0]
    assert m % tm == 0
    batch, earlier = stack if stack is not None else (1, [])
    steps = m // batch // tm
    slots = len(earlier) + 1

    def kv_spec(name, wd, n):
        if name in FEATURE_MAJOR:
            return pl.BlockSpec((n, None, wd, tm), lambda i: (0, i // steps, 0, i % steps))
        return pl.BlockSpec((n, tm, wd), lambda i: (0, i, 0))

    shapes, specs = [], []
    for name, wd, dt in PROJ_OUTS:
        if stack is None or name not in KV_OUTS:
            shapes.append(jax.ShapeDtypeStruct((m, wd), dt))
            specs.append(pl.BlockSpec((tm, wd), lambda i: (i, 0)))
        else:
            shape = (batch, wd, m // batch) if name in FEATURE_MAJOR else (m, wd)
            shapes.append(jax.ShapeDtypeStruct((slots,) + shape, dt))
            specs.append(kv_spec(name, wd, slots))
    widths = {name: wd for name, wd, _ in PROJ_OUTS}
    return pl.pallas_call(
        functools.partial(_proj_kernel, earlier=len(earlier),
                          feature_major=FEATURE_MAJOR if stack is not None else ()),
        out_shape=shapes,
        grid=(m // tm,),
        in_specs=[pl.BlockSpec((tm, D_MODEL), lambda i: (i, 0)),
                  _resident((1, D_MODEL)), _resident(wt.shape)]
        + [kv_spec(name, widths[name], None) for _ in earlier for name in KV_OUTS],
        out_specs=specs,
        compiler_params=_params(("parallel",)),
        name="proj",
    )(x, nw, wt, *[a for layer_kv in earlier for a in layer_kv])


def _merge_kernel(x_ref, osb_ref, odf_ref, ogd_ref, gates_ref, wb_ref, wo_ref, o_ref):
    m = None
    for i, b_ref in enumerate((osb_ref, odf_ref, ogd_ref)):
        p = _dot(b_ref[...], wb_ref[i])
        t = _sigmoid(gates_ref[:, i * D_MODEL:(i + 1) * D_MODEL]) * p
        m = t if m is None else m + t
    o_ref[...] = x_ref[...] + _dot(m.astype(BF16), wo_ref[...])


def _merge(x, o_sb, o_df, o_gd, gates, wb, wo, *, tm):
    m = x.shape[0]
    assert m % tm == 0
    row = lambda w: pl.BlockSpec((tm, w), lambda i: (i, 0))
    return pl.pallas_call(
        _merge_kernel,
        out_shape=jax.ShapeDtypeStruct((m, D_MODEL), F32),
        grid=(m // tm,),
        in_specs=[row(D_MODEL), row(BRANCH_W), row(BRANCH_W), row(BRANCH_W),
                  row(N_BRANCH * D_MODEL), _resident(wb.shape), _resident(wo.shape)],
        out_specs=row(D_MODEL),
        compiler_params=_params(("parallel",)),
        name="merge",
    )(x, o_sb, o_df, o_gd, gates, wb, wo)


def _log_sigmoid_pair(z):
    ls = jnp.minimum(z, 0.0) - jnp.log(1.0 + jnp.exp(-jnp.abs(z)))
    return ls, ls - z


def _suffix_sums(x, upper):
    hi = x.astype(BF16)
    lo = (x - hi.astype(F32)).astype(BF16)
    return _dot(jnp.concatenate([hi, lo], axis=1), jnp.concatenate([upper, upper], axis=0))


ATT_TK = 256
ATT_TQ = 512
DF_BLOCKS_PER_TRIP = 4
SB_DONE = -110.0


def _stack_halves(q, seg):
    lane = lax.broadcasted_iota(jnp.int32, q.shape, 1)
    return jnp.concatenate(
        [jnp.where(lane // seg == j, q, 0.0) for j in range(LANES // seg)], axis=0).astype(BF16)


def _walk_key_blocks(qi, tq, blocks, state, per_trip):
    nd = tq // ATT_TK
    state = blocks([(qi + 1) * nd - 1 - d for d in range(nd)], state, True)
    n_free = qi * nd

    def full_trip(i, st):
        kj = n_free - 1 - per_trip * i
        return blocks([kj - d for d in range(per_trip)], st, False)

    def last_one(i, st):
        return blocks([n_free % per_trip - 1 - i], st, False)

    state = lax.fori_loop(0, n_free // per_trip, full_trip, state)
    return lax.fori_loop(0, n_free % per_trip, last_one, state)


def _sb_prompt_kernel(q_ref, k_ref, v_ref, o_ref, kb_ref, vb_ref, *, tq):
    qi = pl.program_id(2)
    tk = ATT_TK
    n_sub = tq // tk

    @pl.when(qi == 0)
    def _():
        for j in range(kb_ref.shape[0]):
            kb_ref[j] = k_ref[:, j * tk:(j + 1) * tk].astype(BF16)
            vb_ref[j] = v_ref[:, j * tk:(j + 1) * tk].astype(BF16)

    q = q_ref[...].astype(F32) * (DH_SB ** -0.5)
    qss = [_stack_halves(q[g * tk:(g + 1) * tk], DH_SB) for g in range(n_sub)]
    rows = 2 * tk
    rowq = lax.broadcasted_iota(jnp.int32, (rows, tk), 0) % tk
    col = lax.broadcasted_iota(jnp.int32, (rows, tk), 1)
    before = col < rowq
    ur = lax.broadcasted_iota(jnp.int32, (tk, tk), 0)
    uc = lax.broadcasted_iota(jnp.int32, (tk, tk), 1)
    upper = jnp.where(ur > uc, 1.0, 0.0).astype(BF16)

    def fold(qs_list, kjs, states, diagonal):
        pairs = [_log_sigmoid_pair(_dot(qs, kb_ref[kj])) for qs, kj in zip(qs_list, kjs)]
        lss = [p[0] for p in pairs]
        l1ms = [jnp.where(before, p[1], 0.0) if diagonal else p[1] for p in pairs]
        sufs = [_suffix_sums(x, upper) for x in l1ms]
        probs = [jnp.exp(ls + suf + st[0]) for ls, suf, st in zip(lss, sufs, states)]
        if diagonal:
            probs = [jnp.where(before, a, 0.0) for a in probs]
        return [(st[0] + jnp.sum(x, axis=-1, keepdims=True),
                 st[1] + _dot_nt(a.astype(BF16), vb_ref[kj]))
                for st, x, a, kj in zip(states, l1ms, probs, kjs)]

    own = [qi * n_sub + g for g in range(n_sub)]
    zero = (jnp.zeros((rows, 1), F32), jnp.zeros((rows, LANES), F32))
    states = fold(qss, own, [zero] * n_sub, True)
    behind = fold(qss, [jnp.maximum(kj - 1, 0) for kj in own], states, False)
    states = [tuple(jnp.where(own[g] > 0, new, old) for new, old in zip(behind[g], states[g]))
              for g in range(n_sub)]

    def unfinished(loop):
        kj, carry, _ = loop
        return jnp.logical_and(kj >= 0, jnp.max(carry) > SB_DONE)

    lane = lax.broadcasted_iota(jnp.int32, (tk, LANES), 1)
    for g in range(n_sub):
        def one_block(loop, g=g):
            kj, carry, acc = loop
            (carry, acc), = fold([qss[g]], [kj], [(carry, acc)], False)
            return kj - 1, carry, acc

        _, _, acc = lax.while_loop(unfinished, one_block, (own[g] - 2,) + states[g])
        o_ref[g * tk:(g + 1) * tk, :] = jnp.where(lane < DH_SB, acc[:tk], acc[tk:]).astype(BF16)


def _sb_prompt(sq, sk_t, sv_t, layer, *, tq):
    b, t, _ = sq.shape
    assert tq % ATT_TK == 0 and t % tq == 0
    qspec = pl.BlockSpec((None, tq, LANES), lambda bi, hp, qi: (bi, qi, hp))
    kspec = pl.BlockSpec((None, None, LANES, t), lambda bi, hp, qi: (layer, bi, hp, 0))
    return pl.pallas_call(
        functools.partial(_sb_prompt_kernel, tq=tq),
        out_shape=jax.ShapeDtypeStruct((b, t, BRANCH_W), BF16),
        grid=(b, BRANCH_W // LANES, t // tq),
        in_specs=[qspec, kspec, kspec],
        out_specs=qspec,
        scratch_shapes=[pltpu.VMEM((t // ATT_TK, LANES, ATT_TK), BF16)] * 2,
        compiler_params=_params(("parallel", "parallel", "arbitrary")),
        name="sb_prompt",
    )(sq, sk_t, sv_t)


def _diff_lambda(lv, lam_init):
    a = jnp.sum(lv[0:1] * lv[1:2], axis=-1, keepdims=True)
    b = jnp.sum(lv[2:3] * lv[3:4], axis=-1, keepdims=True)
    return jnp.exp(a) - jnp.exp(b) + lam_init


def _df_prompt_kernel(slope_ref, q_ref, k_ref, v_ref, lv_ref, nw_ref, o_ref, kb_ref, vb_ref,
                      *, tq, lam_init):
    h = pl.program_id(1)
    qi = pl.program_id(2)

    tk = ATT_TK
    slope = slope_ref[h]

    @pl.when(qi == 0)
    def _():
        vb_ref[...] = v_ref[...].astype(BF16)
        lane = lax.broadcasted_iota(jnp.int32, (LANES, tk), 1)
        feat = lax.broadcasted_iota(jnp.int32, (LANES, tk), 0)
        for j in range(kb_ref.shape[0]):
            kb_ref[j, 0:LANES, :] = k_ref[:, j * tk:(j + 1) * tk].astype(BF16)
            hi, lo = _two_terms((lane + j * tk).astype(F32) * slope)
            kb_ref[j, LANES:2 * LANES, :] = jnp.where(
                feat == 0, hi, jnp.where(feat == 1, lo, jnp.where(feat == 2, 1.0, 0.0).astype(BF16)))

    qs = _stack_halves(q_ref[...].astype(F32) * (DH_DF ** -0.5), DH_DF)
    rows = qs.shape[0]
    qlane = lax.broadcasted_iota(jnp.int32, (rows, LANES), 1)
    q_bias = jnp.where(qlane < 2, 1.0, jnp.where(qlane == 2, -slope * (qi * tq).astype(F32), 0.0))
    qs = jnp.concatenate([qs, q_bias.astype(BF16)], axis=1)
    rowq = lax.broadcasted_iota(jnp.int32, (rows, tk), 0) % tq
    col = lax.broadcasted_iota(jnp.int32, (rows, tk), 1)

    def block(kj, state, masked):
        m, l, acc = state
        start = pl.multiple_of(kj * tk, tk)
        offset = qi * tq - kj * tk
        s = _dot(qs, kb_ref[kj])
        if masked:
            visible = col - offset <= rowq
            s = jnp.where(visible, s, NEG_BIG)
        m_new = jnp.maximum(m, jnp.max(s, axis=-1, keepdims=True))
        p = jnp.exp(s - m_new)
        if masked:
            p = jnp.where(visible, p, 0.0)
        alpha = jnp.exp(m - m_new)
        l = alpha * l + jnp.sum(p, axis=-1, keepdims=True)
        acc = alpha * acc + _dot(p.astype(BF16), vb_ref[pl.ds(start, tk), :])
        return m_new, l, acc

    def blocks(kjs, state, masked):
        for kj in kjs:
            state = block(kj, state, masked)
        return state

    _, l, acc = _walk_key_blocks(
        qi, tq, blocks, (jnp.full((rows, 1), NEG_BIG, F32), jnp.zeros((rows, 1), F32),
                         jnp.zeros((rows, LANES), F32)), DF_BLOCKS_PER_TRIP)
    lam = _diff_lambda(lv_ref[...], lam_init)
    o = acc[:tq] / l[:tq] - lam * (acc[tq:] / l[tq:])
    o_ref[...] = (_rms(o, nw_ref[...]) * (1.0 - lam_init)).astype(BF16)


def _df_prompt(dq, dk_t, dv, layer, slopes, lv, nw, *, tq, lam_init):
    b, t, _ = dq.shape
    assert tq % ATT_TK == 0 and t % tq == 0
    qspec = pl.BlockSpec((None, tq, LANES), lambda bi, h, qi: (bi, qi, h))
    kspec = pl.BlockSpec((None, None, LANES, t), lambda bi, h, qi: (layer, bi, h, 0))
    vspec = pl.BlockSpec((None, None, t, LANES), lambda bi, h, qi: (layer, bi, 0, h))
    return pl.pallas_call(
        functools.partial(_df_prompt_kernel, tq=tq, lam_init=lam_init),
        out_shape=jax.ShapeDtypeStruct((b, t, BRANCH_W), BF16),
        grid=(b, H_DF, t // tq),
        in_specs=[pl.BlockSpec(memory_space=pltpu.SMEM), qspec, kspec, vspec,
                  _resident(lv.shape), _resident(nw.shape)],
        out_specs=qspec,
        scratch_shapes=[pltpu.VMEM((t // ATT_TK, 2 * LANES, ATT_TK), BF16),
                        pltpu.VMEM((t, LANES), BF16)],
        compiler_params=_params(("parallel", "parallel", "arbitrary")),
        name="df_prompt",
    )(slopes, dq, dk_t, dv, lv, nw)


def _gdn_gates(gab, prm):
    g = -jnp.exp(prm[0:1, :]) * _softplus(gab + prm[1:2, :])
    return g, _sigmoid(gab)


def _l2n(x):
    return x * lax.rsqrt(jnp.sum(x * x, axis=-1, keepdims=True) + L2_EPS)


def _two_terms(x):
    hi = x.astype(BF16)
    return hi, (x - hi.astype(F32)).astype(BF16)


def _split_lhs(a):
    hi, lo = _two_terms(a)
    return jnp.concatenate([hi, hi, lo], axis=1)


def _split_rhs(b):
    hi, lo = _two_terms(b)
    return jnp.concatenate([hi, lo, hi], axis=0)


def _gdn_out(o, nw, z):
    return (_rms(o, nw) * _silu(z)).astype(BF16)


def _gdn_prompt_kernel(xc_ref, gab_ref, gz_ref, cw_ref, prm_ref, nw_ref, cb_ref, s0_ref,
                       o_ref, sfin_ref, buf_ref, st_ref, *, n_steps):
    c = pl.program_id(1)
    ch = GDN_CHUNK
    rows = GDN_CHUNKS_PER_STEP * ch

    @pl.when(c == 0)
    def _():
        buf_ref[0:8, :] = cb_ref[...]
        st_ref[...] = s0_ref[...]

    buf_ref[8:8 + rows, :] = xc_ref[...]
    xconv = None
    for j in range(CONV_W):
        t = cw_ref[j:j + 1, :] * buf_ref[5 + j:5 + j + rows, :]
        xconv = t if xconv is None else xconv + t
    buf_ref[0:8, :] = buf_ref[rows:rows + 8, :]
    xs = _silu(xconv)

    g_all, beta_all = _gdn_gates(gab_ref[...], prm_ref[...])
    row = lax.broadcasted_iota(jnp.int32, (ch, ch), 0)
    col = lax.broadcasted_iota(jnp.int32, (ch, ch), 1)
    tri = col <= row
    strict = col < row
    eye = jnp.where(row == col, 1.0, 0.0)
    lower = jnp.where(tri, 1.0, 0.0)
    hk = H_GD * DK_GD

    problems = [(j, h) for j in range(GDN_CHUNKS_PER_STEP) for h in range(H_GD)]
    cums = []
    for j in range(GDN_CHUNKS_PER_STEP):
        cum_all = _dot(lower, g_all[j * ch:(j + 1) * ch], _HI)
        cums.append((cum_all, cum_all.T))

    def first_stage(j, h):
        r0 = j * ch
        q = _l2n(xs[r0:r0 + ch, h * DK_GD:(h + 1) * DK_GD]) * (DK_GD ** -0.5)
        k = _l2n(xs[r0:r0 + ch, hk + h * DK_GD:hk + (h + 1) * DK_GD])
        v = xs[r0:r0 + ch, 2 * hk + h * DV_GD:2 * hk + (h + 1) * DV_GD]
        gc = cums[j][0][:, h:h + 1]
        gr = cums[j][1][h:h + 1, :]
        beta = beta_all[r0:r0 + ch, H_GD + h:H_GD + h + 1]
        decay = jnp.exp(jnp.where(tri, gc - gr, NEG_BIG))
        eg = jnp.exp(gc)
        g_last = gc[ch - 1:ch, :]
        kb = k.astype(BF16)
        qk = (_dot_nt(q.astype(BF16), kb) * decay).astype(BF16)
        pw = -jnp.where(strict, beta * decay * _dot_nt(kb, kb), 0.0)
        rhs = _split_rhs(jnp.concatenate([beta * v, (beta * eg) * k], axis=1))
        carry_over = ((q * eg).astype(BF16), qk, (k * jnp.exp(g_last - gc)).astype(BF16),
                      jnp.exp(g_last))
        return pw, rhs, carry_over

    staged = [first_stage(j, h) for j, h in problems]
    pws = [p[0] for p in staged]
    invs = [eye + pw for pw in pws]
    for _ in range(5):
        pws = [_dot(_split_lhs(pw), _split_rhs(pw)) for pw in pws]
        invs = [inv + _dot(_split_lhs(inv), _split_rhs(pw)) for inv, pw in zip(invs, pws)]
    sols = [_dot(_split_lhs(inv), p[1]) for inv, p in zip(invs, staged)]
    prepared = {jh: (sol[:, :DV_GD], sol[:, DV_GD:].astype(BF16)) + p[2]
                for jh, sol, p in zip(problems, sols, staged)}

    states = [st_ref[h] for h in range(H_GD)]
    for j in range(GDN_CHUNKS_PER_STEP):
        for h in range(H_GD):
            sol_v, sol_k, q_eg, qk, k_dec, e_last = prepared[(j, h)]
            s = states[h]
            sb = s.astype(BF16)
            ub = (sol_v - _dot(sol_k, sb)).astype(BF16)
            o = _dot(q_eg, sb) + _dot(qk, ub)
            states[h] = e_last * s + _dot_tn(k_dec, ub)
            o_ref[j * ch:(j + 1) * ch, h * DV_GD:(h + 1) * DV_GD] = _gdn_out(
                o, nw_ref[...], gz_ref[j * ch:(j + 1) * ch, h * DV_GD:(h + 1) * DV_GD])
    for h in range(H_GD):
        st_ref[h] = states[h]

    @pl.when(c == n_steps - 1)
    def _():
        sfin_ref[...] = st_ref[...]


def _gdn_prompt(xc, gab, gz, cw, prm, nw, cb8, s0):
    b, t, _ = xc.shape
    rows = GDN_CHUNKS_PER_STEP * GDN_CHUNK
    assert t % rows == 0
    n_steps = t // rows
    tok = lambda w: pl.BlockSpec((None, rows, w), lambda bi, c: (bi, c, 0))
    st = pl.BlockSpec((None, H_GD, DK_GD, DV_GD), lambda bi, c: (bi, 0, 0, 0))
    return pl.pallas_call(
        functools.partial(_gdn_prompt_kernel, n_steps=n_steps),
        out_shape=[jax.ShapeDtypeStruct((b, t, BRANCH_W), BF16),
                   jax.ShapeDtypeStruct((b, H_GD, DK_GD, DV_GD), F32)],
        grid=(b, n_steps),
        in_specs=[tok(GD_CONV_CH), tok(GAB_W), tok(BRANCH_W), _resident(cw.shape),
                  _resident(prm.shape), _resident(nw.shape),
                  pl.BlockSpec((None, 8, GD_CONV_CH), lambda bi, c: (bi, 0, 0)), st],
        out_specs=[tok(BRANCH_W), st],
        scratch_shapes=[pltpu.VMEM((rows + 8, GD_CONV_CH), F32),
                        pltpu.VMEM((H_GD, DK_GD, DV_GD), F32)],
        compiler_params=_params(("parallel", "arbitrary")),
        name="gdn_prompt",
    )(xc, gab, gz, cw, prm, nw, cb8, s0)


GDN_SEQS_PER_STEP = 4


def _gdn_step_kernel(xc_ref, gab_ref, gz_ref, cw_ref, prm_ref, nw_ref, cb_ref, s0_ref,
                     o_ref, sn_ref):
    first = lax.broadcasted_iota(jnp.int32, (8, DK_GD), 0) == 0
    rows8 = lambda x: jnp.where(first, jnp.broadcast_to(x, (8, x.shape[-1])), 0.0)
    hk = H_GD * DK_GD
    for i in range(xc_ref.shape[0]):
        xconv = cw_ref[CONV_W - 1:CONV_W, :] * xc_ref[i]
        for j in range(CONV_W - 1):
            xconv = xconv + cw_ref[j:j + 1, :] * cb_ref[i, j:j + 1, :]
        xs = _silu(xconv)
        g_all, beta_all = _gdn_gates(gab_ref[i], prm_ref[...])
        for h in range(H_GD):
            q = rows8(_l2n(xs[:, h * DK_GD:(h + 1) * DK_GD]) * (DK_GD ** -0.5))
            k = rows8(_l2n(xs[:, hk + h * DK_GD:hk + (h + 1) * DK_GD]))
            v = xs[:, 2 * hk + h * DV_GD:2 * hk + (h + 1) * DV_GD]
            eg = jnp.exp(g_all[:, h:h + 1])
            beta = beta_all[:, H_GD + h:H_GD + h + 1]
            s = s0_ref[i, h]
            u = beta * (v - eg * _dot(k, s, _HI))
            s_new = eg * s + _dot_tn(k, u, _HI)
            sn_ref[i, h] = s_new
            o = _dot(q, s_new, _HI)[0:1, :]
            o_ref[i, :, h * DV_GD:(h + 1) * DV_GD] = _gdn_out(
                o, nw_ref[...], gz_ref[i, :, h * DV_GD:(h + 1) * DV_GD])


def _gdn_step(xc, gab, gz, cw, prm, nw, cb, s0):
    b = xc.shape[0]
    n = math.gcd(b, GDN_SEQS_PER_STEP)
    tok = lambda w: pl.BlockSpec((n, 1, w), lambda bi: (bi, 0, 0))
    st = pl.BlockSpec((n, H_GD, DK_GD, DV_GD), lambda bi: (bi, 0, 0, 0))
    return pl.pallas_call(
        _gdn_step_kernel,
        out_shape=[jax.ShapeDtypeStruct((b, 1, BRANCH_W), BF16),
                   jax.ShapeDtypeStruct((b, H_GD, DK_GD, DV_GD), F32)],
        grid=(b // n,),
        in_specs=[tok(GD_CONV_CH), tok(GAB_W), tok(BRANCH_W), _resident(cw.shape),
                  _resident(prm.shape), _resident(nw.shape),
                  pl.BlockSpec((n, CONV_W - 1, GD_CONV_CH), lambda bi: (bi, 0, 0)), st],
        out_specs=[tok(BRANCH_W), st],
        compiler_params=_params(("parallel",)),
        name="gdn_step",
    )(xc, gab, gz, cw, prm, nw, cb, s0)


PAGES_PER_STEP = 16


def _page_specs(layer, n_pages, block, descending):
    specs = []
    for i in range(PAGES_PER_STEP):
        def index(b, g, pt, i=i):
            p = g * PAGES_PER_STEP + i
            if descending:
                p = n_pages - 1 - p
            return (layer, pt[b, p]) + (0,) * len(block)
        specs.append(pl.BlockSpec((None, None) + block, index))
    return specs


def _store_lane_columns(q, qc_ref):
    n_seg, seg, lanes = qc_ref.shape
    first = lax.broadcasted_iota(jnp.int32, (8, seg), 0) == 0
    ones = jnp.where(lax.broadcasted_iota(jnp.int32, (8, lanes), 0) == 0, 1.0, 0.0).astype(BF16)
    for s in range(n_seg):
        qs = jnp.where(first, jnp.broadcast_to(q[:, s * seg:(s + 1) * seg], (8, seg)), 0.0)
        qc_ref[s] = _dot_tn(qs.astype(BF16), ones)


def _page_scores(k_ref, qc_ref):
    return jnp.concatenate(
        [jnp.sum(k_ref[s] * qc_ref[s], axis=0, keepdims=True) for s in range(k_ref.shape[0])],
        axis=0)


def _head_rows(q, n_rows, seg):
    w = q.shape[-1]
    r = lax.broadcasted_iota(jnp.int32, (n_rows, w), 0)
    c = lax.broadcasted_iota(jnp.int32, (n_rows, w), 1)
    return jnp.where(c // seg == r, jnp.broadcast_to(q, (n_rows, w)), 0.0)


SB_PAGES_PER_GROUP = 2


def _sb_decode_kernel(pt_ref, q_ref, k_hbm, v_hbm, o_ref, kbuf, vbuf, sems, qc_ref, acc_ref,
                      *, layer, n_pages):
    b = pl.program_id(0)
    grp = SB_PAGES_PER_GROUP
    n_groups = n_pages // grp
    page_rows = qc_ref.shape[-1]

    def copies(g):
        slot = g % 2
        out = []
        for i in range(grp):
            page = pt_ref[b, n_pages - 1 - (g * grp + i)]
            out.append(pltpu.make_async_copy(k_hbm.at[layer, page], kbuf.at[slot, i], sems.at[slot, 0, i]))
            out.append(pltpu.make_async_copy(v_hbm.at[layer, page], vbuf.at[slot, i], sems.at[slot, 1, i]))
        return out

    for c in copies(0):
        c.start()
    _store_lane_columns(q_ref[...].astype(F32) * (DH_SB ** -0.5), qc_ref)
    acc_ref[...] = jnp.zeros(acc_ref.shape, F32)
    row = lax.broadcasted_iota(jnp.int32, (page_rows, page_rows), 0)
    col = lax.broadcasted_iota(jnp.int32, (page_rows, page_rows), 1)
    upper = jnp.where(row > col, 1.0, 0.0).astype(BF16)

    def unfinished(loop):
        g, carry = loop
        return jnp.logical_and(g < n_groups, jnp.max(carry) > SB_DONE)

    def one_group(loop):
        g, carry = loop
        slot = g % 2

        @pl.when(g + 1 < n_groups)
        def _():
            for c in copies(g + 1):
                c.start()

        for c in copies(g):
            c.wait()
        z = jnp.concatenate([_page_scores(kbuf.at[slot, i], qc_ref) for i in range(grp)], axis=0)
        ls, l1m = _log_sigmoid_pair(z)
        page_sums = jnp.sum(l1m, axis=-1, keepdims=True)
        carries = [carry]
        for i in range(grp):
            carries.append(carries[-1] + page_sums[i * H_SB:(i + 1) * H_SB])
        a = jnp.exp(ls + _suffix_sums(l1m, upper) + jnp.concatenate(carries[:-1], axis=0))
        for h in range(H_SB):
            t = acc_ref[h]
            for i in range(grp):
                t = t + vbuf[slot, i, h] * a[i * H_SB + h:i * H_SB + h + 1, :]
            acc_ref[h] = t
        return g + 1, carries[-1]

    g_end, _ = lax.while_loop(unfinished, one_group, (0, jnp.zeros((H_SB, 1), F32)))

    @pl.when(g_end < n_groups)
    def _():
        for c in copies(g_end):
            c.wait()

    ones = jnp.ones((8, page_rows), F32)
    o_ref[...] = jnp.concatenate(
        [_dot_nt(ones, acc_ref[h], _HI)[0:1, :] for h in range(H_SB)], axis=1).astype(BF16)


def _sb_decode(sq, cache_k, cache_v, page_table, layer):
    b = sq.shape[0]
    n_pages = page_table.shape[1]
    block = cache_k.shape[2:]
    grp = SB_PAGES_PER_GROUP
    assert n_pages % grp == 0
    tok = pl.BlockSpec((None, 1, BRANCH_W), lambda bi, pt: (bi, 0, 0))
    hbm = pl.BlockSpec(memory_space=pl.ANY)
    return pl.pallas_call(
        functools.partial(_sb_decode_kernel, layer=layer, n_pages=n_pages),
        out_shape=jax.ShapeDtypeStruct((b, 1, BRANCH_W), BF16),
        grid_spec=pltpu.PrefetchScalarGridSpec(
            num_scalar_prefetch=1, grid=(b,),
            in_specs=[tok, hbm, hbm], out_specs=tok,
            scratch_shapes=[pltpu.VMEM((2, grp) + block, F32), pltpu.VMEM((2, grp) + block, F32),
                            pltpu.SemaphoreType.DMA((2, 2, grp)),
                            pltpu.VMEM(block, F32), pltpu.VMEM(block, F32)]),
        compiler_params=_params(("arbitrary",)),
        name="sb_decode",
    )(page_table, sq, cache_k, cache_v)


def _df_decode_kernel(pt_ref, q_ref, kn_ref, vn_ref, slope_ref, lv_ref, nw_ref, *refs,
                      n_steps, lam_init):
    k_refs = refs[:PAGES_PER_STEP]
    v_refs = refs[PAGES_PER_STEP:2 * PAGES_PER_STEP]
    o_ref, qc_ref, m_ref, l_ref, acc_ref = refs[2 * PAGES_PER_STEP:]
    g = pl.program_id(1)
    n_maps = 2 * H_DF
    page_rows = qc_ref.shape[-1]
    past = n_steps * PAGES_PER_STEP * page_rows
    q = q_ref[...].astype(F32) * (DH_DF ** -0.5)

    @pl.when(g == 0)
    def _():
        _store_lane_columns(q, qc_ref)
        m_ref[...] = jnp.full(m_ref.shape, NEG_BIG, F32)
        l_ref[...] = jnp.zeros(l_ref.shape, F32)
        acc_ref[...] = jnp.zeros(acc_ref.shape, F32)

    rows = PAGES_PER_STEP * n_maps
    slope = jnp.concatenate([slope_ref[...][:, 0:1]] * PAGES_PER_STEP, axis=0)
    page_of_row = lax.broadcasted_iota(jnp.int32, (rows, page_rows), 0) // n_maps
    within = lax.broadcasted_iota(jnp.int32, (rows, page_rows), 1)
    first_key = (g * PAGES_PER_STEP + page_of_row) * page_rows
    dist = (past - first_key - within).astype(F32)
    flat = H_DF * page_rows
    ek = lax.broadcasted_iota(jnp.int32, (page_rows, flat), 0)
    er = lax.broadcasted_iota(jnp.int32, (page_rows, flat), 1)
    expand = jnp.where(er // H_DF == ek, 1.0, 0.0).astype(BF16)
    mr = lax.broadcasted_iota(jnp.int32, (rows, flat), 0)
    mc = lax.broadcasted_iota(jnp.int32, (rows, flat), 1)
    own_head = mc % H_DF == (mr % n_maps) // 2
    m, l, acc = m_ref[...], l_ref[...], acc_ref[...]

    def over_pages(op, x):
        return functools.reduce(op, [x[i * n_maps:(i + 1) * n_maps] for i in range(PAGES_PER_STEP)])

    s = jnp.concatenate([_page_scores(k_ref, qc_ref) for k_ref in k_refs], axis=0) - slope * dist
    m_new = jnp.maximum(m, over_pages(jnp.maximum, jnp.max(s, axis=-1, keepdims=True)))
    p = jnp.exp(s - jnp.concatenate([m_new] * PAGES_PER_STEP, axis=0))
    alpha = jnp.exp(m - m_new)
    pe = jnp.where(own_head, _dot(p.astype(BF16), expand), 0.0)
    pe_all = jnp.concatenate(
        [pe[i * n_maps:(i + 1) * n_maps] for i in range(PAGES_PER_STEP)], axis=1).astype(BF16)
    v_all = jnp.concatenate(
        [v_ref[...].reshape(flat, DV_DF_W).astype(BF16) for v_ref in v_refs], axis=0)
    acc = alpha * acc + _dot(pe_all, v_all)
    m_ref[...] = m_new
    l_ref[...] = alpha * l + over_pages(jnp.add, jnp.sum(p, axis=-1, keepdims=True))
    acc_ref[...] = acc

    @pl.when(g == n_steps - 1)
    def _():
        s_own = jnp.sum(_head_rows(q, n_maps, DH_DF) * kn_ref[...], axis=-1, keepdims=True)
        v_own = jnp.concatenate(
            [vn_ref[:, (r // 2) * DV_DF_W:(r // 2 + 1) * DV_DF_W] for r in range(n_maps)], axis=0)
        m_last, l_last, acc_last = m_ref[...], l_ref[...], acc_ref[...]
        mf = jnp.maximum(m_last, s_own)
        p_own = jnp.exp(s_own - mf)
        scale_last = jnp.exp(m_last - mf)
        lf = scale_last * l_last + p_own
        accf = scale_last * acc_last + p_own * v_own
        lam = _diff_lambda(lv_ref[...], lam_init)
        o_map = accf / lf
        for h in range(H_DF):
            o = o_map[2 * h:2 * h + 1, :] - lam * o_map[2 * h + 1:2 * h + 2, :]
            o_ref[:, h * DV_DF_W:(h + 1) * DV_DF_W] = (
                _rms(o, nw_ref[...]) * (1.0 - lam_init)).astype(BF16)


DV_DF_W = 2 * DH_DF


def _df_decode(dq, dk_new, dv_new, cache_k, cache_v, page_table, slopes8, lv, nw, layer,
               lam_init):
    b = dq.shape[0]
    n_pages = page_table.shape[1]
    kblock, vblock = cache_k.shape[2:], cache_v.shape[2:]
    assert n_pages % PAGES_PER_STEP == 0
    n_steps = n_pages // PAGES_PER_STEP
    tok = pl.BlockSpec((None, 1, BRANCH_W), lambda bi, g, pt: (bi, 0, 0))
    const = lambda shape: pl.BlockSpec(shape, lambda bi, g, pt: (0,) * len(shape))
    n_maps = 2 * H_DF
    return pl.pallas_call(
        functools.partial(_df_decode_kernel, n_steps=n_steps, lam_init=lam_init),
        out_shape=jax.ShapeDtypeStruct((b, 1, BRANCH_W), BF16),
        grid_spec=pltpu.PrefetchScalarGridSpec(
            num_scalar_prefetch=1, grid=(b, n_steps),
            in_specs=[tok, tok, tok, const(slopes8.shape), const(lv.shape), const(nw.shape)]
            + _page_specs(layer, n_pages, kblock, False)
            + _page_specs(layer, n_pages, vblock, False),
            out_specs=tok,
            scratch_shapes=[pltpu.VMEM(kblock, F32), pltpu.VMEM((n_maps, 1), F32),
                            pltpu.VMEM((n_maps, 1), F32), pltpu.VMEM((n_maps, DV_DF_W), F32)]),
        compiler_params=_params(("parallel", "arbitrary")),
        name="df_decode",
    )(page_table, dq, dk_new, dv_new, slopes8, lv, nw,
      *([cache_k] * PAGES_PER_STEP), *([cache_v] * PAGES_PER_STEP))


def _cache_views(cache_sb_k, cache_sb_v, cache_df_k, cache_df_v):
    d, n, p = cache_sb_k.shape[:3]
    keys_last = lambda c: jnp.transpose(
        c.reshape(d, n, p, BRANCH_W // DH_SB, DH_SB), (0, 1, 3, 4, 2))
    return (keys_last(cache_sb_k), keys_last(cache_sb_v), keys_last(cache_df_k),
            cache_df_v.reshape(d, n, p * H_DF // 8, 8, DV_DF_W))


def _lane_row(v):
    return jnp.pad(v.astype(F32), (0, LANES - v.shape[0]))[None, :]


def _row_tile(m):
    return min(m, 256)


def kernel(x_prompt, x_sample, cache_sb_k, cache_sb_v, cache_df_k, cache_df_v, state_gdn,
           state_gdn_conv, page_table, norm_w, ffn_w_in, ffn_w_out, w_in, diff_lambda,
           diff_norm_w, gdn_conv_w, gdn_a_log, gdn_dt_bias, gdn_norm_w, w_branch, w_out,
           final_norm_w):
    depth = w_in.shape[0]
    bp, tp, _ = x_prompt.shape
    bs, ts, _ = x_sample.shape
    assert ts == 1

    ffn_in = ffn_w_in.astype(BF16)
    ffn_out = ffn_w_out.astype(BF16)
    wb = w_branch.astype(BF16)
    wo = w_out.astype(BF16)
    fw = final_norm_w[None, :]
    slopes = 2.0 ** (-8.0 * jnp.arange(1, H_DF + 1, dtype=F32) / H_DF)
    slopes8 = jnp.broadcast_to(jnp.repeat(slopes, 2)[:, None], (2 * H_DF, LANES))
    ck, cv, dkc, dvc = _cache_views(cache_sb_k, cache_sb_v, cache_df_k, cache_df_v)

    xp = x_prompt.reshape(bp * tp, D_MODEL)
    xs = x_sample.reshape(bs, D_MODEL)
    tmp, tms = _row_tile(bp * tp), _row_tile(bs)
    tmp_wide = min(bp * tp, 2 * tmp)
    p_states, s_states, p_kv = [], [], []
    for l in range(depth):
        lam_init = 0.8 - 0.6 * math.exp(-0.3 * l)
        wt = jnp.transpose(w_in[l]).astype(BF16)
        nw = norm_w[l][:, None, :]
        prm = jnp.concatenate([_lane_row(gdn_a_log[l]), _lane_row(gdn_dt_bias[l]),
                               jnp.zeros((6, LANES), F32)], axis=0)
        gnw = gdn_norm_w[l][None, :]
        dnw = diff_norm_w[l][None, :]
        last = l == depth - 1

        xp = _ffn(xp, nw[0], ffn_in[l, 0], ffn_out[l, 0], fw, final=False, tm=tmp_wide)
        sq, sk_t, sv_t, dq, dk_t, dv, xc, gz, gates, gab = _proj(
            xp, nw[1], wt, tm=tmp, stack=(bp, p_kv if last else []))
        p_kv.append((sk_t, sv_t, dk_t, dv))
        slot = sk_t.shape[0] - 1
        r3 = lambda a: a.reshape(bp, tp, a.shape[-1])
        o_sb = _sb_prompt(r3(sq), sk_t, sv_t, slot, tq=ATT_TQ)
        o_df = _df_prompt(r3(dq), dk_t, dv.reshape(slot + 1, bp, tp, BRANCH_W), slot, slopes,
                          diff_lambda[l], dnw, tq=ATT_TQ, lam_init=lam_init)
        xc3 = r3(xc)
        o_gd, s_fin = _gdn_prompt(
            xc3, r3(gab), r3(gz), gdn_conv_w[l], prm, gnw,
            jnp.zeros((bp, 8, GD_CONV_CH), F32),
            jnp.zeros((bp, H_GD, DK_GD, DV_GD), F32))
        xp = _merge(xp, o_sb.reshape(bp * tp, BRANCH_W), o_df.reshape(bp * tp, BRANCH_W),
                    o_gd.reshape(bp * tp, BRANCH_W), gates, wb[l], wo[l], tm=tmp_wide)
        xp = _ffn(xp, nw[2], ffn_in[l, 1], ffn_out[l, 1], fw, final=last, tm=tmp_wide)
        p_states.append((s_fin, xc3[:, tp - (CONV_W - 1):, :]))

        xs = _ffn(xs, nw[0], ffn_in[l, 0], ffn_out[l, 0], fw, final=False, tm=tms)
        sq, sk, sv, dq, dk, dv, xc, gz, gates, gab = _proj(xs, nw[1], wt, tm=tms)
        r3 = lambda a: a.reshape(bs, 1, a.shape[-1])
        o_sb = _sb_decode(r3(sq), ck, cv, page_table, l)
        o_df = _df_decode(r3(dq), r3(dk), r3(dv), dkc, dvc, page_table, slopes8,
                          diff_lambda[l], dnw, l, lam_init)
        o_gd, s_new = _gdn_step(r3(xc), r3(gab), r3(gz), gdn_conv_w[l], prm, gnw,
                                state_gdn_conv[l], state_gdn[l])
        xs = _merge(xs, o_sb.reshape(bs, BRANCH_W), o_df.reshape(bs, BRANCH_W),
                    o_gd.reshape(bs, BRANCH_W), gates, wb[l], wo[l], tm=tms)
        xs = _ffn(xs, nw[2], ffn_in[l, 1], ffn_out[l, 1], fw, final=last, tm=tms)
        new_buf = jnp.concatenate([state_gdn_conv[l][:, 1:, :], r3(xc)], axis=1)
        s_states.append((
            sk.reshape(bs, 1, H_SB, DH_SB), sv.reshape(bs, 1, H_SB, DH_SB),
            dk.reshape(bs, 1, H_DF, 2, DH_DF), dv.reshape(bs, 1, H_DF, DV_DF_W),
            s_new, new_buf))

    sk_t, sv_t, dk_t, dv = p_kv[-1]
    heads_last = lambda a: jnp.transpose(a.reshape(depth, bp, H_SB, DH_SB, tp), (0, 1, 4, 2, 3))
    p_kv_out = [heads_last(sk_t), heads_last(sv_t),
                jnp.transpose(dk_t.reshape(depth, bp, H_DF, 2, DH_DF, tp), (0, 1, 5, 2, 3, 4)),
                dv.reshape(depth, bp, tp, H_DF, DV_DF_W)]
    p_out = [jnp.stack(a) for a in zip(*p_states)]
    s_out = [jnp.stack(a) for a in zip(*s_states)]
    return (xp.reshape(bp, tp, D_MODEL), xs.reshape(bs, 1, D_MODEL), *p_kv_out, *p_out, *s_out)
```

```python
import functools
import math

import jax
import jax.numpy as jnp
from jax import lax
from jax.experimental import pallas as pl
from jax.experimental.pallas import tpu as pltpu

F32 = jnp.float32
BF16 = jnp.bfloat16

D_MODEL = 1024
DH_SB = 64
H_SB = 8
DH_DF = 64
H_DF = 4
DK_GD = 128
DV_GD = 128
H_GD = 4
CONV_W = 4
GD_CONV_CH = 2 * H_GD * DK_GD + H_GD * DV_GD
GDN_CHUNK = 64
GDN_CHUNKS_PER_STEP = 4
BRANCH_W = 512
N_BRANCH = 3
D_FF = 2816
NORM_EPS = 1e-6
L2_EPS = 1e-6
NEG_BIG = -1e30

LANES = 128
FF_CHUNK = 256
GAB_W = LANES
VMEM_LIMIT = 56 * 1024 * 1024

PROJ_OUTS = (
    ("sq", 512, BF16), ("sk", 512, F32), ("sv", 512, F32),
    ("dq", 512, BF16), ("dk", 512, F32), ("dv", 512, F32),
    ("xc", GD_CONV_CH, F32), ("gz", 512, F32),
    ("gates", N_BRANCH * D_MODEL, F32), ("gab", GAB_W, F32),
)


def _dot(a, b, precision=None):
    return jnp.dot(a, b, preferred_element_type=F32, precision=precision)


def _dot_nt(a, b, precision=None):
    return lax.dot_general(a, b, (((1,), (1,)), ((), ())),
                           preferred_element_type=F32, precision=precision)


def _dot_tn(a, b, precision=None):
    return lax.dot_general(a, b, (((0,), (0,)), ((), ())),
                           preferred_element_type=F32, precision=precision)


_HI = lax.Precision.HIGHEST


def _rms(x, w):
    ms = jnp.mean(x * x, axis=-1, keepdims=True)
    return x * lax.rsqrt(ms + NORM_EPS) * w


def _sigmoid(x):
    return 1.0 / (1.0 + jnp.exp(-x))


def _silu(x):
    return x * _sigmoid(x)


def _softplus(x):
    return jnp.maximum(x, 0.0) + jnp.log(1.0 + jnp.exp(-jnp.abs(x)))


def _params(sem):
    return pltpu.CompilerParams(dimension_semantics=sem, vmem_limit_bytes=VMEM_LIMIT)


def _resident(shape, at=()):
    block = (None,) * len(at) + tuple(shape[len(at):])
    index = tuple(at) + (0,) * (len(shape) - len(at))
    return pl.BlockSpec(block, lambda *_: index, pipeline_mode=pl.Buffered(1))


def _ffn_kernel(x_ref, nw_ref, win_ref, wout_ref, fw_ref, o_ref, *, final):
    x = x_ref[...]
    h = _rms(x, nw_ref[...]).astype(BF16)
    acc = jnp.zeros(x.shape, F32)
    for c in range(D_FF // FF_CHUNK):
        lo = c * FF_CHUNK
        g = _dot(h, win_ref[:, lo:lo + FF_CHUNK])
        u = _dot(h, win_ref[:, D_FF + lo:D_FF + lo + FF_CHUNK])
        a = (_silu(g) * u).astype(BF16)
        acc = acc + _dot(a, wout_ref[lo:lo + FF_CHUNK, :])
    y = x + 0.5 * acc
    if final:
        y = _rms(y, fw_ref[...])
    o_ref[...] = y


def _ffn(x, nw, w_in, w_out, fw, *, final, tm, at=()):
    m = x.shape[0]
    assert m % tm == 0
    row = pl.BlockSpec((tm, D_MODEL), lambda i: (i, 0))
    return pl.pallas_call(
        functools.partial(_ffn_kernel, final=final),
        out_shape=jax.ShapeDtypeStruct((m, D_MODEL), F32),
        grid=(m // tm,),
        in_specs=[row, _resident((1, D_MODEL)), _resident(w_in.shape, at),
                  _resident(w_out.shape, at), _resident((1, D_MODEL))],
        out_specs=row,
        compiler_params=_params(("parallel",)),
        name="ffn",
    )(x, nw, w_in, w_out, fw)


W_IN_ROWS = {"sq": (0, 512), "sk": (512, 1024), "sv": (1024, 1536), "dq": (1536, 2048),
             "dk": (2048, 2560), "dv": (2560, 3072), "xc": (3072, 4608), "gab": (4608, 4616),
             "gz": (4616, 5128), "gates": (5128, 8200)}
KV_OUTS = ("sk", "sv", "dk", "dv")
FEATURE_MAJOR = ("sk", "sv", "dk")


def _proj_kernel(x_ref, nw_ref, wt_ref, *refs, feature_major, earlier):
    o_refs = refs[len(refs) - len(PROJ_OUTS):]
    kv_in = refs[:len(refs) - len(PROJ_OUTS)]
    h = _rms(x_ref[...], nw_ref[...]).astype(BF16)
    for (name, width, dt), o_ref in zip(PROJ_OUTS, o_refs):
        r0, r1 = W_IN_ROWS[name]
        if name in KV_OUTS and feature_major:
            new = (_dot_nt(wt_ref[r0:r1, :], h) if name in feature_major
                   else _dot_nt(h, wt_ref[r0:r1, :])).astype(dt)
            for j in range(earlier):
                o_ref[j] = kv_in[j * len(KV_OUTS) + KV_OUTS.index(name)][...]
            o_ref[earlier] = new
        elif name == "gab":
            g = _dot_nt(h, wt_ref[r0:r1, :])
            o_ref[...] = jnp.concatenate(
                [g, jnp.zeros((g.shape[0], width - g.shape[1]), F32)], axis=1)
        else:
            for lo in range(0, width, 512):
                o_ref[:, lo:lo + 512] = _dot_nt(h, wt_ref[r0 + lo:r0 + lo + 512, :]).astype(dt)


def _proj(x, nw, wt, *, tm, stack=None):
    m = x.shape[0]
    assert m % tm == 0
    batch, earlier = stack if stack is not None else (1, [])
    steps = m // batch // tm
    slots = len(earlier) + 1

    def kv_spec(name, wd, n):
        if name in FEATURE_MAJOR:
            return pl.BlockSpec((n, None, wd, tm), lambda i: (0, i // steps, 0, i % steps))
        return pl.BlockSpec((n, tm, wd), lambda i: (0, i, 0))

    shapes, specs = [], []
    for name, wd, dt in PROJ_OUTS:
        if stack is None or name not in KV_OUTS:
            shapes.append(jax.ShapeDtypeStruct((m, wd), dt))
            specs.append(pl.BlockSpec((tm, wd), lambda i: (i, 0)))
        else:
            shape = (batch, wd, m // batch) if name in FEATURE_MAJOR else (m, wd)
            shapes.append(jax.ShapeDtypeStruct((slots,) + shape, dt))
            specs.append(kv_spec(name, wd, slots))
    widths = {name: wd for name, wd, _ in PROJ_OUTS}
    return pl.pallas_call(
        functools.partial(_proj_kernel, earlier=len(earlier),
                          feature_major=FEATURE_MAJOR if stack is not None else ()),
        out_shape=shapes,
        grid=(m // tm,),
        in_specs=[pl.BlockSpec((tm, D_MODEL), lambda i: (i, 0)),
                  _resident((1, D_MODEL)), _resident(wt.shape)]
        + [kv_spec(name, widths[name], None) for _ in earlier for name in KV_OUTS],
        out_specs=specs,
        compiler_params=_params(("parallel",)),
        name="proj",
    )(x, nw, wt, *[a for layer_kv in earlier for a in layer_kv])


def _merge_kernel(x_ref, osb_ref, odf_ref, ogd_ref, gates_ref, wb_ref, wo_ref, o_ref):
    m = None
    for i, b_ref in enumerate((osb_ref, odf_ref, ogd_ref)):
        p = _dot(b_ref[...], wb_ref[i])
        t = _sigmoid(gates_ref[:, i * D_MODEL:(i + 1) * D_MODEL]) * p
        m = t if m is None else m + t
    o_ref[...] = x_ref[...] + _dot(m.astype(BF16), wo_ref[...])


def _merge(x, o_sb, o_df, o_gd, gates, wb, wo, *, tm, at=()):
    m = x.shape[0]
    assert m % tm == 0
    row = lambda w: pl.BlockSpec((tm, w), lambda i: (i, 0))
    return pl.pallas_call(
        _merge_kernel,
        out_shape=jax.ShapeDtypeStruct((m, D_MODEL), F32),
        grid=(m // tm,),
        in_specs=[row(D_MODEL), row(BRANCH_W), row(BRANCH_W), row(BRANCH_W),
                  row(N_BRANCH * D_MODEL), _resident(wb.shape, at), _resident(wo.shape, at)],
        out_specs=row(D_MODEL),
        compiler_params=_params(("parallel",)),
        name="merge",
    )(x, o_sb, o_df, o_gd, gates, wb, wo)


def _log_sigmoid_pair(z):
    ls = jnp.minimum(z, 0.0) - jnp.log(1.0 + jnp.exp(-jnp.abs(z)))
    return ls, ls - z


def _suffix_sums(x, upper):
    hi = x.astype(BF16)
    lo = (x - hi.astype(F32)).astype(BF16)
    return _dot(jnp.concatenate([hi, lo], axis=1), jnp.concatenate([upper, upper], axis=0))


ATT_TK = 256
ATT_TQ = 512
DF_BLOCKS_PER_TRIP = 4
SB_DONE = -110.0


def _stack_halves(q, seg):
    lane = lax.broadcasted_iota(jnp.int32, q.shape, 1)
    return jnp.concatenate(
        [jnp.where(lane // seg == j, q, 0.0) for j in range(LANES // seg)], axis=0).astype(BF16)


def _walk_key_blocks(qi, tq, blocks, state, per_trip):
    nd = tq // ATT_TK
    state = blocks([(qi + 1) * nd - 1 - d for d in range(nd)], state, True)
    n_free = qi * nd

    def full_trip(i, st):
        kj = n_free - 1 - per_trip * i
        return blocks([kj - d for d in range(per_trip)], st, False)

    def last_one(i, st):
        return blocks([n_free % per_trip - 1 - i], st, False)

    state = lax.fori_loop(0, n_free // per_trip, full_trip, state)
    return lax.fori_loop(0, n_free % per_trip, last_one, state)


def _sb_prompt_kernel(q_ref, k_ref, v_ref, o_ref, kb_ref, vb_ref, *, tq):
    qi = pl.program_id(2)
    tk = ATT_TK
    n_sub = tq // tk

    @pl.when(qi == 0)
    def _():
        for j in range(kb_ref.shape[0]):
            kb_ref[j] = k_ref[:, j * tk:(j + 1) * tk].astype(BF16)
            vb_ref[j] = v_ref[:, j * tk:(j + 1) * tk].astype(BF16)

    q = q_ref[...].astype(F32) * (DH_SB ** -0.5)
    qss = [_stack_halves(q[g * tk:(g + 1) * tk], DH_SB) for g in range(n_sub)]
    rows = 2 * tk
    rowq = lax.broadcasted_iota(jnp.int32, (rows, tk), 0) % tk
    col = lax.broadcasted_iota(jnp.int32, (rows, tk), 1)
    before = col < rowq
    ur = lax.broadcasted_iota(jnp.int32, (tk, tk), 0)
    uc = lax.broadcasted_iota(jnp.int32, (tk, tk), 1)
    upper = jnp.where(ur > uc, 1.0, 0.0).astype(BF16)

    def fold(qs_list, kjs, states, diagonal):
        pairs = [_log_sigmoid_pair(_dot(qs, kb_ref[kj])) for qs, kj in zip(qs_list, kjs)]
        lss = [p[0] for p in pairs]
        l1ms = [jnp.where(before, p[1], 0.0) if diagonal else p[1] for p in pairs]
        sufs = [_suffix_sums(x, upper) for x in l1ms]
        probs = [jnp.exp(ls + suf + st[0]) for ls, suf, st in zip(lss, sufs, states)]
        if diagonal:
            probs = [jnp.where(before, a, 0.0) for a in probs]
        return [(st[0] + jnp.sum(x, axis=-1, keepdims=True),
                 st[1] + _dot_nt(a.astype(BF16), vb_ref[kj]))
                for st, x, a, kj in zip(states, l1ms, probs, kjs)]

    own = [qi * n_sub + g for g in range(n_sub)]
    zero = (jnp.zeros((rows, 1), F32), jnp.zeros((rows, LANES), F32))
    states = fold(qss, own, [zero] * n_sub, True)
    behind = fold(qss, [jnp.maximum(kj - 1, 0) for kj in own], states, False)
    states = [tuple(jnp.where(own[g] > 0, new, old) for new, old in zip(behind[g], states[g]))
              for g in range(n_sub)]

    def unfinished(loop):
        kj, carry, _ = loop
        return jnp.logical_and(kj >= 0, jnp.max(carry) > SB_DONE)

    lane = lax.broadcasted_iota(jnp.int32, (tk, LANES), 1)
    for g in range(n_sub):
        def one_block(loop, g=g):
            kj, carry, acc = loop
            (carry, acc), = fold([qss[g]], [kj], [(carry, acc)], False)
            return kj - 1, carry, acc

        _, _, acc = lax.while_loop(unfinished, one_block, (own[g] - 2,) + states[g])
        o_ref[g * tk:(g + 1) * tk, :] = jnp.where(lane < DH_SB, acc[:tk], acc[tk:]).astype(BF16)


def _sb_prompt(sq, sk_t, sv_t, layer, *, tq):
    b, t, _ = sq.shape
    assert tq % ATT_TK == 0 and t % tq == 0
    qspec = pl.BlockSpec((None, tq, LANES), lambda bi, hp, qi: (bi, qi, hp))
    kspec = pl.BlockSpec((None, None, LANES, t), lambda bi, hp, qi: (layer, bi, hp, 0))
    return pl.pallas_call(
        functools.partial(_sb_prompt_kernel, tq=tq),
        out_shape=jax.ShapeDtypeStruct((b, t, BRANCH_W), BF16),
        grid=(b, BRANCH_W // LANES, t // tq),
        in_specs=[qspec, kspec, kspec],
        out_specs=qspec,
        scratch_shapes=[pltpu.VMEM((t // ATT_TK, LANES, ATT_TK), BF16)] * 2,
        compiler_params=_params(("parallel", "parallel", "arbitrary")),
        name="sb_prompt",
    )(sq, sk_t, sv_t)


def _diff_lambda(lv, lam_init):
    a = jnp.sum(lv[0:1] * lv[1:2], axis=-1, keepdims=True)
    b = jnp.sum(lv[2:3] * lv[3:4], axis=-1, keepdims=True)
    return jnp.exp(a) - jnp.exp(b) + lam_init


def _df_prompt_kernel(slope_ref, q_ref, k_ref, v_ref, lv_ref, nw_ref, o_ref, kb_ref, vb_ref,
                      *, tq, lam_init):
    h = pl.program_id(1)
    qi = pl.program_id(2)

    tk = ATT_TK
    slope = slope_ref[h]

    @pl.when(qi == 0)
    def _():
        vb_ref[...] = v_ref[...].astype(BF16)
        lane = lax.broadcasted_iota(jnp.int32, (LANES, tk), 1)
        feat = lax.broadcasted_iota(jnp.int32, (LANES, tk), 0)
        for j in range(kb_ref.shape[0]):
            kb_ref[j, 0:LANES, :] = k_ref[:, j * tk:(j + 1) * tk].astype(BF16)
            hi, lo = _two_terms((lane + j * tk).astype(F32) * slope)
            kb_ref[j, LANES:2 * LANES, :] = jnp.where(
                feat == 0, hi, jnp.where(feat == 1, lo, jnp.where(feat == 2, 1.0, 0.0).astype(BF16)))

    qs = _stack_halves(q_ref[...].astype(F32) * (DH_DF ** -0.5), DH_DF)
    rows = qs.shape[0]
    qlane = lax.broadcasted_iota(jnp.int32, (rows, LANES), 1)
    q_bias = jnp.where(qlane < 2, 1.0, jnp.where(qlane == 2, -slope * (qi * tq).astype(F32), 0.0))
    qs = jnp.concatenate([qs, q_bias.astype(BF16)], axis=1)
    rowq = lax.broadcasted_iota(jnp.int32, (rows, tk), 0) % tq
    col = lax.broadcasted_iota(jnp.int32, (rows, tk), 1)

    def block(kj, state, masked):
        m, l, acc = state
        start = pl.multiple_of(kj * tk, tk)
        offset = qi * tq - kj * tk
        s = _dot(qs, kb_ref[kj])
        if masked:
            visible = col - offset <= rowq
            s = jnp.where(visible, s, NEG_BIG)
        m_new = jnp.maximum(m, jnp.max(s, axis=-1, keepdims=True))
        p = jnp.exp(s - m_new)
        if masked:
            p = jnp.where(visible, p, 0.0)
        alpha = jnp.exp(m - m_new)
        l = alpha * l + jnp.sum(p, axis=-1, keepdims=True)
        acc = alpha * acc + _dot(p.astype(BF16), vb_ref[pl.ds(start, tk), :])
        return m_new, l, acc

    def blocks(kjs, state, masked):
        for kj in kjs:
            state = block(kj, state, masked)
        return state

    _, l, acc = _walk_key_blocks(
        qi, tq, blocks, (jnp.full((rows, 1), NEG_BIG, F32), jnp.zeros((rows, 1), F32),
                         jnp.zeros((rows, LANES), F32)), DF_BLOCKS_PER_TRIP)
    lam = _diff_lambda(lv_ref[...], lam_init)
    o = acc[:tq] / l[:tq] - lam * (acc[tq:] / l[tq:])
    o_ref[...] = (_rms(o, nw_ref[...]) * (1.0 - lam_init)).astype(BF16)


def _df_prompt(dq, dk_t, dv, layer, slopes, lv, nw, *, tq, lam_init):
    b, t, _ = dq.shape
    assert tq % ATT_TK == 0 and t % tq == 0
    qspec = pl.BlockSpec((None, tq, LANES), lambda bi, h, qi: (bi, qi, h))
    kspec = pl.BlockSpec((None, None, LANES, t), lambda bi, h, qi: (layer, bi, h, 0))
    vspec = pl.BlockSpec((None, None, t, LANES), lambda bi, h, qi: (layer, bi, 0, h))
    return pl.pallas_call(
        functools.partial(_df_prompt_kernel, tq=tq, lam_init=lam_init),
        out_shape=jax.ShapeDtypeStruct((b, t, BRANCH_W), BF16),
        grid=(b, H_DF, t // tq),
        in_specs=[pl.BlockSpec(memory_space=pltpu.SMEM), qspec, kspec, vspec,
                  _resident(lv.shape), _resident(nw.shape)],
        out_specs=qspec,
        scratch_shapes=[pltpu.VMEM((t // ATT_TK, 2 * LANES, ATT_TK), BF16),
                        pltpu.VMEM((t, LANES), BF16)],
        compiler_params=_params(("parallel", "parallel", "arbitrary")),
        name="df_prompt",
    )(slopes, dq, dk_t, dv, lv, nw)


def _gdn_gates(gab, prm):
    g = -jnp.exp(prm[0:1, :]) * _softplus(gab + prm[1:2, :])
    return g, _sigmoid(gab)


def _l2n(x):
    return x * lax.rsqrt(jnp.sum(x * x, axis=-1, keepdims=True) + L2_EPS)


def _two_terms(x):
    hi = x.astype(BF16)
    return hi, (x - hi.astype(F32)).astype(BF16)


def _split_lhs(a):
    hi, lo = _two_terms(a)
    return jnp.concatenate([hi, hi, lo], axis=1)


def _split_rhs(b):
    hi, lo = _two_terms(b)
    return jnp.concatenate([hi, lo, hi], axis=0)


def _gdn_out(o, nw, z):
    return (_rms(o, nw) * _silu(z)).astype(BF16)


def _gdn_prompt_kernel(xc_ref, gab_ref, gz_ref, cw_ref, prm_ref, nw_ref, cb_ref, s0_ref,
                       o_ref, sfin_ref, buf_ref, st_ref, *, n_steps):
    c = pl.program_id(1)
    ch = GDN_CHUNK
    rows = GDN_CHUNKS_PER_STEP * ch

    @pl.when(c == 0)
    def _():
        buf_ref[0:8, :] = cb_ref[...]
        st_ref[...] = s0_ref[...]

    buf_ref[8:8 + rows, :] = xc_ref[...]
    xconv = None
    for j in range(CONV_W):
        t = cw_ref[j:j + 1, :] * buf_ref[5 + j:5 + j + rows, :]
        xconv = t if xconv is None else xconv + t
    buf_ref[0:8, :] = buf_ref[rows:rows + 8, :]
    xs = _silu(xconv)

    g_all, beta_all = _gdn_gates(gab_ref[...], prm_ref[...])
    row = lax.broadcasted_iota(jnp.int32, (ch, ch), 0)
    col = lax.broadcasted_iota(jnp.int32, (ch, ch), 1)
    tri = col <= row
    strict = col < row
    eye = jnp.where(row == col, 1.0, 0.0)
    lower = jnp.where(tri, 1.0, 0.0)
    hk = H_GD * DK_GD

    problems = [(j, h) for j in range(GDN_CHUNKS_PER_STEP) for h in range(H_GD)]
    cums = []
    for j in range(GDN_CHUNKS_PER_STEP):
        cum_all = _dot(lower, g_all[j * ch:(j + 1) * ch], _HI)
        cums.append((cum_all, cum_all.T))

    def first_stage(j, h):
        r0 = j * ch
        q = _l2n(xs[r0:r0 + ch, h * DK_GD:(h + 1) * DK_GD]) * (DK_GD ** -0.5)
        k = _l2n(xs[r0:r0 + ch, hk + h * DK_GD:hk + (h + 1) * DK_GD])
        v = xs[r0:r0 + ch, 2 * hk + h * DV_GD:2 * hk + (h + 1) * DV_GD]
        gc = cums[j][0][:, h:h + 1]
        gr = cums[j][1][h:h + 1, :]
        beta = beta_all[r0:r0 + ch, H_GD + h:H_GD + h + 1]
        decay = jnp.exp(jnp.where(tri, gc - gr, NEG_BIG))
        eg = jnp.exp(gc)
        g_last = gc[ch - 1:ch, :]
        kb = k.astype(BF16)
        qk = (_dot_nt(q.astype(BF16), kb) * decay).astype(BF16)
        pw = -jnp.where(strict, beta * decay * _dot_nt(kb, kb), 0.0)
        rhs = _split_rhs(jnp.concatenate([beta * v, (beta * eg) * k], axis=1))
        carry_over = ((q * eg).astype(BF16), qk, (k * jnp.exp(g_last - gc)).astype(BF16),
                      jnp.exp(g_last))
        return pw, rhs, carry_over

    staged = [first_stage(j, h) for j, h in problems]
    pws = [p[0] for p in staged]
    invs = [eye + pw for pw in pws]
    for _ in range(5):
        pws = [_dot(_split_lhs(pw), _split_rhs(pw)) for pw in pws]
        invs = [inv + _dot(_split_lhs(inv), _split_rhs(pw)) for inv, pw in zip(invs, pws)]
    sols = [_dot(_split_lhs(inv), p[1]) for inv, p in zip(invs, staged)]
    prepared = {jh: (sol[:, :DV_GD], sol[:, DV_GD:].astype(BF16)) + p[2]
                for jh, sol, p in zip(problems, sols, staged)}

    states = [st_ref[h] for h in range(H_GD)]
    for j in range(GDN_CHUNKS_PER_STEP):
        for h in range(H_GD):
            sol_v, sol_k, q_eg, qk, k_dec, e_last = prepared[(j, h)]
            s = states[h]
            sb = s.astype(BF16)
            ub = (sol_v - _dot(sol_k, sb)).astype(BF16)
            o = _dot(q_eg, sb) + _dot(qk, ub)
            states[h] = e_last * s + _dot_tn(k_dec, ub)
            o_ref[j * ch:(j + 1) * ch, h * DV_GD:(h + 1) * DV_GD] = _gdn_out(
                o, nw_ref[...], gz_ref[j * ch:(j + 1) * ch, h * DV_GD:(h + 1) * DV_GD])
    for h in range(H_GD):
        st_ref[h] = states[h]

    @pl.when(c == n_steps - 1)
    def _():
        sfin_ref[...] = st_ref[...]


def _gdn_prompt(xc, gab, gz, cw, prm, nw, cb8, s0):
    b, t, _ = xc.shape
    rows = GDN_CHUNKS_PER_STEP * GDN_CHUNK
    assert t % rows == 0
    n_steps = t // rows
    tok = lambda w: pl.BlockSpec((None, rows, w), lambda bi, c: (bi, c, 0))
    st = pl.BlockSpec((None, H_GD, DK_GD, DV_GD), lambda bi, c: (bi, 0, 0, 0))
    return pl.pallas_call(
        functools.partial(_gdn_prompt_kernel, n_steps=n_steps),
        out_shape=[jax.ShapeDtypeStruct((b, t, BRANCH_W), BF16),
                   jax.ShapeDtypeStruct((b, H_GD, DK_GD, DV_GD), F32)],
        grid=(b, n_steps),
        in_specs=[tok(GD_CONV_CH), tok(GAB_W), tok(BRANCH_W), _resident(cw.shape),
                  _resident(prm.shape), _resident(nw.shape),
                  pl.BlockSpec((None, 8, GD_CONV_CH), lambda bi, c: (bi, 0, 0)), st],
        out_specs=[tok(BRANCH_W), st],
        scratch_shapes=[pltpu.VMEM((rows + 8, GD_CONV_CH), F32),
                        pltpu.VMEM((H_GD, DK_GD, DV_GD), F32)],
        compiler_params=_params(("parallel", "arbitrary")),
        name="gdn_prompt",
    )(xc, gab, gz, cw, prm, nw, cb8, s0)


GDN_SEQS_PER_STEP = 4


def _gdn_step_kernel(xc_ref, gab_ref, gz_ref, cw_ref, prm_ref, nw_ref, cb_ref, s0_ref,
                     o_ref, sn_ref):
    first = lax.broadcasted_iota(jnp.int32, (8, DK_GD), 0) == 0
    rows8 = lambda x: jnp.where(first, jnp.broadcast_to(x, (8, x.shape[-1])), 0.0)
    hk = H_GD * DK_GD
    for i in range(xc_ref.shape[0]):
        xconv = cw_ref[CONV_W - 1:CONV_W, :] * xc_ref[i]
        for j in range(CONV_W - 1):
            xconv = xconv + cw_ref[j:j + 1, :] * cb_ref[i, j:j + 1, :]
        xs = _silu(xconv)
        g_all, beta_all = _gdn_gates(gab_ref[i], prm_ref[...])
        for h in range(H_GD):
            q = rows8(_l2n(xs[:, h * DK_GD:(h + 1) * DK_GD]) * (DK_GD ** -0.5))
            k = rows8(_l2n(xs[:, hk + h * DK_GD:hk + (h + 1) * DK_GD]))
            v = xs[:, 2 * hk + h * DV_GD:2 * hk + (h + 1) * DV_GD]
            eg = jnp.exp(g_all[:, h:h + 1])
            beta = beta_all[:, H_GD + h:H_GD + h + 1]
            s = s0_ref[i, h]
            u = beta * (v - eg * _dot(k, s, _HI))
            s_new = eg * s + _dot_tn(k, u, _HI)
            sn_ref[i, h] = s_new
            o = _dot(q, s_new, _HI)[0:1, :]
            o_ref[i, :, h * DV_GD:(h + 1) * DV_GD] = _gdn_out(
                o, nw_ref[...], gz_ref[i, :, h * DV_GD:(h + 1) * DV_GD])


def _gdn_step(xc, gab, gz, cw, prm, nw, cb, s0):
    b = xc.shape[0]
    n = math.gcd(b, GDN_SEQS_PER_STEP)
    tok = lambda w: pl.BlockSpec((n, 1, w), lambda bi: (bi, 0, 0))
    st = pl.BlockSpec((n, H_GD, DK_GD, DV_GD), lambda bi: (bi, 0, 0, 0))
    return pl.pallas_call(
        _gdn_step_kernel,
        out_shape=[jax.ShapeDtypeStruct((b, 1, BRANCH_W), BF16),
                   jax.ShapeDtypeStruct((b, H_GD, DK_GD, DV_GD), F32)],
        grid=(b // n,),
        in_specs=[tok(GD_CONV_CH), tok(GAB_W), tok(BRANCH_W), _resident(cw.shape),
                  _resident(prm.shape), _resident(nw.shape),
                  pl.BlockSpec((n, CONV_W - 1, GD_CONV_CH), lambda bi: (bi, 0, 0)), st],
        out_specs=[tok(BRANCH_W), st],
        compiler_params=_params(("parallel",)),
        name="gdn_step",
    )(xc, gab, gz, cw, prm, nw, cb, s0)


PAGES_PER_STEP = 16


def _page_specs(layer, n_pages, block, descending):
    specs = []
    for i in range(PAGES_PER_STEP):
        def index(b, g, pt, i=i):
            p = g * PAGES_PER_STEP + i
            if descending:
                p = n_pages - 1 - p
            return (layer, pt[b, p]) + (0,) * len(block)
        specs.append(pl.BlockSpec((None, None) + block, index))
    return specs


def _store_lane_columns(q, qc_ref):
    n_seg, seg, lanes = qc_ref.shape
    first = lax.broadcasted_iota(jnp.int32, (8, seg), 0) == 0
    ones = jnp.where(lax.broadcasted_iota(jnp.int32, (8, lanes), 0) == 0, 1.0, 0.0).astype(BF16)
    for s in range(n_seg):
        qs = jnp.where(first, jnp.broadcast_to(q[:, s * seg:(s + 1) * seg], (8, seg)), 0.0)
        qc_ref[s] = _dot_tn(qs.astype(BF16), ones)


def _page_scores(k_ref, qc_ref):
    return jnp.concatenate(
        [jnp.sum(k_ref[s] * qc_ref[s], axis=0, keepdims=True) for s in range(k_ref.shape[0])],
        axis=0)


def _head_rows(q, n_rows, seg):
    w = q.shape[-1]
    r = lax.broadcasted_iota(jnp.int32, (n_rows, w), 0)
    c = lax.broadcasted_iota(jnp.int32, (n_rows, w), 1)
    return jnp.where(c // seg == r, jnp.broadcast_to(q, (n_rows, w)), 0.0)


SB_PAGES_PER_GROUP = 2


def _sb_decode_kernel(pt_ref, q_ref, k_hbm, v_hbm, o_ref, kbuf, vbuf, sems, qc_ref, acc_ref,
                      *, layer, n_pages):
    b = pl.program_id(0)
    grp = SB_PAGES_PER_GROUP
    n_groups = n_pages // grp
    page_rows = qc_ref.shape[-1]

    def copies(g):
        slot = g % 2
        out = []
        for i in range(grp):
            page = pt_ref[b, n_pages - 1 - (g * grp + i)]
            out.append(pltpu.make_async_copy(k_hbm.at[layer, page], kbuf.at[slot, i], sems.at[slot, 0, i]))
            out.append(pltpu.make_async_copy(v_hbm.at[layer, page], vbuf.at[slot, i], sems.at[slot, 1, i]))
        return out

    for c in copies(0):
        c.start()
    _store_lane_columns(q_ref[...].astype(F32) * (DH_SB ** -0.5), qc_ref)
    acc_ref[...] = jnp.zeros(acc_ref.shape, F32)
    row = lax.broadcasted_iota(jnp.int32, (page_rows, page_rows), 0)
    col = lax.broadcasted_iota(jnp.int32, (page_rows, page_rows), 1)
    upper = jnp.where(row > col, 1.0, 0.0).astype(BF16)

    def unfinished(loop):
        g, carry = loop
        return jnp.logical_and(g < n_groups, jnp.max(carry) > SB_DONE)

    def one_group(loop):
        g, carry = loop
        slot = g % 2

        @pl.when(g + 1 < n_groups)
        def _():
            for c in copies(g + 1):
                c.start()

        for c in copies(g):
            c.wait()
        z = jnp.concatenate([_page_scores(kbuf.at[slot, i], qc_ref) for i in range(grp)], axis=0)
        ls, l1m = _log_sigmoid_pair(z)
        page_sums = jnp.sum(l1m, axis=-1, keepdims=True)
        carries = [carry]
        for i in range(grp):
            carries.append(carries[-1] + page_sums[i * H_SB:(i + 1) * H_SB])
        a = jnp.exp(ls + _suffix_sums(l1m, upper) + jnp.concatenate(carries[:-1], axis=0))
        for h in range(H_SB):
            t = acc_ref[h]
            for i in range(grp):
                t = t + vbuf[slot, i, h] * a[i * H_SB + h:i * H_SB + h + 1, :]
            acc_ref[h] = t
        return g + 1, carries[-1]

    g_end, _ = lax.while_loop(unfinished, one_group, (0, jnp.zeros((H_SB, 1), F32)))

    @pl.when(g_end < n_groups)
    def _():
        for c in copies(g_end):
            c.wait()

    ones = jnp.ones((8, page_rows), F32)
    o_ref[...] = jnp.concatenate(
        [_dot_nt(ones, acc_ref[h], _HI)[0:1, :] for h in range(H_SB)], axis=1).astype(BF16)


def _sb_decode(sq, cache_k, cache_v, page_table, layer):
    b = sq.shape[0]
    n_pages = page_table.shape[1]
    block = cache_k.shape[2:]
    grp = SB_PAGES_PER_GROUP
    assert n_pages % grp == 0
    tok = pl.BlockSpec((None, 1, BRANCH_W), lambda bi, pt: (bi, 0, 0))
    hbm = pl.BlockSpec(memory_space=pl.ANY)
    return pl.pallas_call(
        functools.partial(_sb_decode_kernel, layer=layer, n_pages=n_pages),
        out_shape=jax.ShapeDtypeStruct((b, 1, BRANCH_W), BF16),
        grid_spec=pltpu.PrefetchScalarGridSpec(
            num_scalar_prefetch=1, grid=(b,),
            in_specs=[tok, hbm, hbm], out_specs=tok,
            scratch_shapes=[pltpu.VMEM((2, grp) + block, F32), pltpu.VMEM((2, grp) + block, F32),
                            pltpu.SemaphoreType.DMA((2, 2, grp)),
                            pltpu.VMEM(block, F32), pltpu.VMEM(block, F32)]),
        compiler_params=_params(("arbitrary",)),
        name="sb_decode",
    )(page_table, sq, cache_k, cache_v)


def _df_decode_kernel(pt_ref, q_ref, kn_ref, vn_ref, slope_ref, lv_ref, nw_ref, *refs,
                      n_steps, lam_init):
    k_refs = refs[:PAGES_PER_STEP]
    v_refs = refs[PAGES_PER_STEP:2 * PAGES_PER_STEP]
    o_ref, qc_ref, m_ref, l_ref, acc_ref = refs[2 * PAGES_PER_STEP:]
    g = pl.program_id(1)
    n_maps = 2 * H_DF
    page_rows = qc_ref.shape[-1]
    past = n_steps * PAGES_PER_STEP * page_rows
    q = q_ref[...].astype(F32) * (DH_DF ** -0.5)

    @pl.when(g == 0)
    def _():
        _store_lane_columns(q, qc_ref)
        m_ref[...] = jnp.full(m_ref.shape, NEG_BIG, F32)
        l_ref[...] = jnp.zeros(l_ref.shape, F32)
        acc_ref[...] = jnp.zeros(acc_ref.shape, F32)

    rows = PAGES_PER_STEP * n_maps
    slope = jnp.concatenate([slope_ref[...][:, 0:1]] * PAGES_PER_STEP, axis=0)
    page_of_row = lax.broadcasted_iota(jnp.int32, (rows, page_rows), 0) // n_maps
    within = lax.broadcasted_iota(jnp.int32, (rows, page_rows), 1)
    first_key = (g * PAGES_PER_STEP + page_of_row) * page_rows
    dist = (past - first_key - within).astype(F32)
    flat = H_DF * page_rows
    ek = lax.broadcasted_iota(jnp.int32, (page_rows, flat), 0)
    er = lax.broadcasted_iota(jnp.int32, (page_rows, flat), 1)
    expand = jnp.where(er // H_DF == ek, 1.0, 0.0).astype(BF16)
    mr = lax.broadcasted_iota(jnp.int32, (rows, flat), 0)
    mc = lax.broadcasted_iota(jnp.int32, (rows, flat), 1)
    own_head = mc % H_DF == (mr % n_maps) // 2
    m, l, acc = m_ref[...], l_ref[...], acc_ref[...]

    def over_pages(op, x):
        return functools.reduce(op, [x[i * n_maps:(i + 1) * n_maps] for i in range(PAGES_PER_STEP)])

    s = jnp.concatenate([_page_scores(k_ref, qc_ref) for k_ref in k_refs], axis=0) - slope * dist
    m_new = jnp.maximum(m, over_pages(jnp.maximum, jnp.max(s, axis=-1, keepdims=True)))
    p = jnp.exp(s - jnp.concatenate([m_new] * PAGES_PER_STEP, axis=0))
    alpha = jnp.exp(m - m_new)
    pe = jnp.where(own_head, _dot(p.astype(BF16), expand), 0.0)
    pe_all = jnp.concatenate(
        [pe[i * n_maps:(i + 1) * n_maps] for i in range(PAGES_PER_STEP)], axis=1).astype(BF16)
    v_all = jnp.concatenate(
        [v_ref[...].reshape(flat, DV_DF_W).astype(BF16) for v_ref in v_refs], axis=0)
    acc = alpha * acc + _dot(pe_all, v_all)
    m_ref[...] = m_new
    l_ref[...] = alpha * l + over_pages(jnp.add, jnp.sum(p, axis=-1, keepdims=True))
    acc_ref[...] = acc

    @pl.when(g == n_steps - 1)
    def _():
        s_own = jnp.sum(_head_rows(q, n_maps, DH_DF) * kn_ref[...], axis=-1, keepdims=True)
        v_own = jnp.concatenate(
            [vn_ref[:, (r // 2) * DV_DF_W:(r // 2 + 1) * DV_DF_W] for r in range(n_maps)], axis=0)
        m_last, l_last, acc_last = m_ref[...], l_ref[...], acc_ref[...]
        mf = jnp.maximum(m_last, s_own)
        p_own = jnp.exp(s_own - mf)
        scale_last = jnp.exp(m_last - mf)
        lf = scale_last * l_last + p_own
        accf = scale_last * acc_last + p_own * v_own
        lam = _diff_lambda(lv_ref[...], lam_init)
        o_map = accf / lf
        for h in range(H_DF):
            o = o_map[2 * h:2 * h + 1, :] - lam * o_map[2 * h + 1:2 * h + 2, :]
            o_ref[:, h * DV_DF_W:(h + 1) * DV_DF_W] = (
                _rms(o, nw_ref[...]) * (1.0 - lam_init)).astype(BF16)


DV_DF_W = 2 * DH_DF


def _df_decode(dq, dk_new, dv_new, cache_k, cache_v, page_table, slopes8, lv, nw, layer,
               lam_init):
    b = dq.shape[0]
    n_pages = page_table.shape[1]
    kblock, vblock = cache_k.shape[2:], cache_v.shape[2:]
    assert n_pages % PAGES_PER_STEP == 0
    n_steps = n_pages // PAGES_PER_STEP
    tok = pl.BlockSpec((None, 1, BRANCH_W), lambda bi, g, pt: (bi, 0, 0))
    const = lambda shape: pl.BlockSpec(shape, lambda bi, g, pt: (0,) * len(shape))
    n_maps = 2 * H_DF
    return pl.pallas_call(
        functools.partial(_df_decode_kernel, n_steps=n_steps, lam_init=lam_init),
        out_shape=jax.ShapeDtypeStruct((b, 1, BRANCH_W), BF16),
        grid_spec=pltpu.PrefetchScalarGridSpec(
            num_scalar_prefetch=1, grid=(b, n_steps),
            in_specs=[tok, tok, tok, const(slopes8.shape), const(lv.shape), const(nw.shape)]
            + _page_specs(layer, n_pages, kblock, False)
            + _page_specs(layer, n_pages, vblock, False),
            out_specs=tok,
            scratch_shapes=[pltpu.VMEM(kblock, F32), pltpu.VMEM((n_maps, 1), F32),
                            pltpu.VMEM((n_maps, 1), F32), pltpu.VMEM((n_maps, DV_DF_W), F32)]),
        compiler_params=_params(("parallel", "arbitrary")),
        name="df_decode",
    )(page_table, dq, dk_new, dv_new, slopes8, lv, nw,
      *([cache_k] * PAGES_PER_STEP), *([cache_v] * PAGES_PER_STEP))


def _cache_views(cache_sb_k, cache_sb_v, cache_df_k, cache_df_v):
    d, n, p = cache_sb_k.shape[:3]
    keys_last = lambda c: jnp.transpose(
        c.reshape(d, n, p, BRANCH_W // DH_SB, DH_SB), (0, 1, 3, 4, 2))
    return (keys_last(cache_sb_k), keys_last(cache_sb_v), keys_last(cache_df_k),
            cache_df_v.reshape(d, n, p * H_DF // 8, 8, DV_DF_W))


def _lane_row(v):
    return jnp.pad(v.astype(F32), (0, LANES - v.shape[0]))[None, :]


def _row_tile(m):
    return min(m, 256)


def kernel(x_prompt, x_sample, cache_sb_k, cache_sb_v, cache_df_k, cache_df_v, state_gdn,
           state_gdn_conv, page_table, norm_w, ffn_w_in, ffn_w_out, w_in, diff_lambda,
           diff_norm_w, gdn_conv_w, gdn_a_log, gdn_dt_bias, gdn_norm_w, w_branch, w_out,
           final_norm_w):
    depth = w_in.shape[0]
    bp, tp, _ = x_prompt.shape
    bs, ts, _ = x_sample.shape
    assert ts == 1

    ffn_in = ffn_w_in.astype(BF16)
    ffn_out = ffn_w_out.astype(BF16)
    wb = w_branch.astype(BF16)
    wo = w_out.astype(BF16)
    fw = final_norm_w[None, :]
    slopes = 2.0 ** (-8.0 * jnp.arange(1, H_DF + 1, dtype=F32) / H_DF)
    slopes8 = jnp.broadcast_to(jnp.repeat(slopes, 2)[:, None], (2 * H_DF, LANES))
    ck, cv, dkc, dvc = _cache_views(cache_sb_k, cache_sb_v, cache_df_k, cache_df_v)

    xp = x_prompt.reshape(bp * tp, D_MODEL)
    xs = x_sample.reshape(bs, D_MODEL)
    tmp, tms = _row_tile(bp * tp), _row_tile(bs)
    tmp_wide = min(bp * tp, 2 * tmp)
    p_states, s_states, p_kv = [], [], []
    for l in range(depth):
        lam_init = 0.8 - 0.6 * math.exp(-0.3 * l)
        wt = jnp.transpose(w_in[l]).astype(BF16)
        nw = norm_w[l][:, None, :]
        prm = jnp.concatenate([_lane_row(gdn_a_log[l]), _lane_row(gdn_dt_bias[l]),
                               jnp.zeros((6, LANES), F32)], axis=0)
        gnw = gdn_norm_w[l][None, :]
        dnw = diff_norm_w[l][None, :]
        last = l == depth - 1

        xp = _ffn(xp, nw[0], ffn_in, ffn_out, fw, at=(l, 0), final=False, tm=tmp_wide)
        sq, sk_t, sv_t, dq, dk_t, dv, xc, gz, gates, gab = _proj(
            xp, nw[1], wt, tm=tmp, stack=(bp, p_kv if last else []))
        p_kv.append((sk_t, sv_t, dk_t, dv))
        slot = sk_t.shape[0] - 1
        r3 = lambda a: a.reshape(bp, tp, a.shape[-1])
        o_sb = _sb_prompt(r3(sq), sk_t, sv_t, slot, tq=ATT_TQ)
        o_df = _df_prompt(r3(dq), dk_t, dv.reshape(slot + 1, bp, tp, BRANCH_W), slot, slopes,
                          diff_lambda[l], dnw, tq=ATT_TQ, lam_init=lam_init)
        xc3 = r3(xc)
        o_gd, s_fin = _gdn_prompt(
            xc3, r3(gab), r3(gz), gdn_conv_w[l], prm, gnw,
            jnp.zeros((bp, 8, GD_CONV_CH), F32),
            jnp.zeros((bp, H_GD, DK_GD, DV_GD), F32))
        xp = _merge(xp, o_sb.reshape(bp * tp, BRANCH_W), o_df.reshape(bp * tp, BRANCH_W),
                    o_gd.reshape(bp * tp, BRANCH_W), gates, wb, wo, at=(l,), tm=tmp_wide)
        xp = _ffn(xp, nw[2], ffn_in, ffn_out, fw, at=(l, 1), final=last, tm=tmp_wide)
        p_states.append((s_fin, xc3[:, tp - (CONV_W - 1):, :]))

        xs = _ffn(xs, nw[0], ffn_in, ffn_out, fw, at=(l, 0), final=False, tm=tms)
        sq, sk, sv, dq, dk, dv, xc, gz, gates, gab = _proj(xs, nw[1], wt, tm=tms)
        r3 = lambda a: a.reshape(bs, 1, a.shape[-1])
        o_sb = _sb_decode(r3(sq), ck, cv, page_table, l)
        o_df = _df_decode(r3(dq), r3(dk), r3(dv), dkc, dvc, page_table, slopes8,
                          diff_lambda[l], dnw, l, lam_init)
        o_gd, s_new = _gdn_step(r3(xc), r3(gab), r3(gz), gdn_conv_w[l], prm, gnw,
                                state_gdn_conv[l], state_gdn[l])
        xs = _merge(xs, o_sb.reshape(bs, BRANCH_W), o_df.reshape(bs, BRANCH_W),
                    o_gd.reshape(bs, BRANCH_W), gates, wb, wo, at=(l,), tm=tms)
        xs = _ffn(xs, nw[2], ffn_in, ffn_out, fw, at=(l, 1), final=last, tm=tms)
        new_buf = jnp.concatenate([state_gdn_conv[l][:, 1:, :], r3(xc)], axis=1)
        s_states.append((
            sk.reshape(bs, 1, H_SB, DH_SB), sv.reshape(bs, 1, H_SB, DH_SB),
            dk.reshape(bs, 1, H_DF, 2, DH_DF), dv.reshape(bs, 1, H_DF, DV_DF_W),
            s_new, new_buf))

    sk_t, sv_t, dk_t, dv = p_kv[-1]
    heads_last = lambda a: jnp.transpose(a.reshape(depth, bp, H_SB, DH_SB, tp), (0, 1, 4, 2, 3))
    p_kv_out = [heads_last(sk_t), heads_last(sv_t),
                jnp.transpose(dk_t.reshape(depth, bp, H_DF, 2, DH_DF, tp), (0, 1, 5, 2, 3, 4)),
                dv.reshape(depth, bp, tp, H_DF, DV_DF_W)]
    p_out = [jnp.stack(a) for a in zip(*p_states)]
    s_out = [jnp.stack(a) for a in zip(*s_states)]
    return (xp.reshape(bp, tp, D_MODEL), xs.reshape(bs, 1, D_MODEL), *p_kv_out, *p_out, *s_out)
```
